```python
import math
import jax, jax.numpy as jnp
from jax import lax
import numpy as np

D_MODEL = 1024
BATCH = 8
SEQ = 4096
DEPTH = 2
DEC_BATCH = 128
DEC_SEQ = 8
PAST_LEN = 16384
PAGE_SIZE = 128

MLA_HEADS = 8
Q_LORA = 384
KV_LORA = 256
NOPE_DIM = 64
ROPE_DIM = 32
V_DIM = 64
ROPE_THETA = 10000.0
Q_BLOCK = 128

GLA_HEADS = 4
GLA_DK = D_MODEL // 2 // GLA_HEADS
GLA_DV = D_MODEL // GLA_HEADS
GLA_LOWRANK = 16
GLA_TAU = 16.0
GLA_CHUNK = 64

D_FF = 4 * D_MODEL
N_ADA = 6
DEEPNORM_ALPHA = (2.0 * DEPTH) ** 0.25
DEEPNORM_BETA = (8.0 * DEPTH) ** -0.25
EPS = 1e-6

IN_SPLITS = (Q_LORA, KV_LORA, ROPE_DIM,
             GLA_HEADS * GLA_DK, GLA_HEADS * GLA_DK, GLA_HEADS * GLA_DV,
             GLA_LOWRANK, GLA_HEADS * GLA_DV, 2 * D_MODEL)
D_IN = sum(IN_SPLITS)

kernel_name = "mla_gla_gated_deepnorm_adaln_step"


def rms_norm(x, g):
    xf = x.astype(jnp.float32)
    y = xf * lax.rsqrt(jnp.mean(xf * xf, axis=-1, keepdims=True) + EPS)
    return (y * g.astype(jnp.float32)).astype(x.dtype)


def layer_norm(x, g=None, b=None):
    xf = x.astype(jnp.float32)
    mu = jnp.mean(xf, axis=-1, keepdims=True)
    var = jnp.mean(jnp.square(xf - mu), axis=-1, keepdims=True)
    y = (xf - mu) * lax.rsqrt(var + EPS)
    if g is not None:
        y = y * g.astype(jnp.float32) + b.astype(jnp.float32)
    return y.astype(x.dtype)


def rope(x, pos):
    half = ROPE_DIM // 2
    inv = ROPE_THETA ** (-jnp.arange(half, dtype=jnp.float32) / half)
    ang = pos.astype(jnp.float32)[:, None] * inv[None, :]
    ang = ang.reshape(ang.shape[:1] + (1,) * (x.ndim - 3) + (half,))
    cos, sin = jnp.cos(ang).astype(x.dtype), jnp.sin(ang).astype(x.dtype)
    x1, x2 = x[..., :half], x[..., half:]
    return jnp.concatenate([x1 * cos - x2 * sin, x1 * sin + x2 * cos], axis=-1)


def mla_attend(q_lat, q_pe, ckv, kpe, q_pos, k_pos):
    scale = (NOPE_DIM + ROPE_DIM) ** -0.5
    s = jnp.einsum('bqhc,btc->bhqt', q_lat, ckv) + jnp.einsum('bqhr,btr->bhqt', q_pe, kpe)
    s = s.astype(jnp.float32) * scale
    s = jnp.where(k_pos[None, :] <= q_pos[:, None], s, -jnp.inf)
    p = jax.nn.softmax(s, axis=-1).astype(ckv.dtype)
    return jnp.einsum('bhqt,btc->bqhc', p, ckv)


def mla_attend_blocked(q_lat, q_pe, ckv, kpe, q_pos, k_pos):
    B, Lq = q_lat.shape[:2]
    blk = Q_BLOCK if Lq % Q_BLOCK == 0 else Lq
    nb = Lq // blk
    ql = q_lat.reshape(B, nb, blk, MLA_HEADS, KV_LORA).swapaxes(0, 1)
    qp = q_pe.reshape(B, nb, blk, MLA_HEADS, ROPE_DIM).swapaxes(0, 1)
    qpos = q_pos.reshape(nb, blk)

    def one(args):
        a, b_, c_ = args
        return mla_attend(a, b_, ckv, kpe, c_, k_pos)

    o = lax.map(one, (ql, qp, qpos))
    return o.swapaxes(0, 1).reshape(B, Lq, MLA_HEADS, KV_LORA)


def gla_chunked(q, k, v, log_a, S0):
    B, L, H, DK = q.shape
    DV = v.shape[-1]
    C = math.gcd(L, GLA_CHUNK)
    nc = L // C

    def to_chunks(t):
        return t.reshape(B, nc, C, H, t.shape[-1]).transpose(1, 0, 3, 2, 4).astype(jnp.float32)

    qc, kc, vc, gc = to_chunks(q), to_chunks(k), to_chunks(v), to_chunks(log_a)
    causal = jnp.tril(jnp.ones((C, C), dtype=bool))[..., None]

    def step(S, inp):
        qi, ki, vi, gi = inp
        b = jnp.cumsum(gi, axis=2)
        b_last = b[:, :, -1:, :]
        diff = b[:, :, :, None, :] - b[:, :, None, :, :]
        decay = jnp.where(causal, jnp.exp(jnp.where(causal, diff, 0.0)), 0.0)
        A = jnp.einsum('bhtd,bhsd,bhtsd->bhts', qi, ki, decay)
        o = jnp.einsum('bhts,bhsv->bhtv', A, vi) + jnp.einsum('bhtd,bhdv->bhtv', qi * jnp.exp(b), S)
        S = jnp.exp(b_last)[:, :, 0, :, None] * S + jnp.einsum('bhsd,bhsv->bhdv', ki * jnp.exp(b_last - b), vi)
        return S, o

    S, o = lax.scan(step, S0.astype(jnp.float32), (qc, kc, vc, gc))
    o = o.transpose(1, 0, 3, 2, 4).reshape(B, L, H, DV)
    return o.astype(v.dtype), S.astype(S0.dtype)


def token_mixer(h, q_pos, past, S0, l, w_in, q_norm, kv_norm, w_uq, w_uk, w_uv, w_mla_br,
                w_gate_up, b_gate, gla_norm, w_gla_br, w_out):
    B, L, _ = h.shape
    idx = [int(i) for i in np.cumsum(IN_SPLITS)[:-1]]
    q_dn, kv_dn, kr, gq, gk, gv, ga, gr, br = jnp.split(h @ w_in[l], idx, axis=-1)

    q = (rms_norm(q_dn, q_norm[l]) @ w_uq[l]).reshape(B, L, MLA_HEADS, NOPE_DIM + ROPE_DIM)
    q_pe = rope(q[..., NOPE_DIM:], q_pos)
    q_lat = jnp.einsum('blhn,chn->blhc', q[..., :NOPE_DIM], w_uk[l])
    ckv_new = rms_norm(kv_dn, kv_norm[l])
    kpe_new = rope(kr, q_pos)
    if past is None:
        ckv, kpe = ckv_new, kpe_new
    else:
        ckv = jnp.concatenate([past[0], ckv_new], axis=1)
        kpe = jnp.concatenate([past[1], kpe_new], axis=1)
    k_pos = jnp.arange(ckv.shape[1], dtype=jnp.int32)
    o_lat = mla_attend_blocked(q_lat, q_pe, ckv, kpe, q_pos, k_pos)
    u_mla = jnp.einsum('blhc,chv->blhv', o_lat, w_uv[l]).reshape(B, L, MLA_HEADS * V_DIM) @ w_mla_br[l]

    gq = gq.reshape(B, L, GLA_HEADS, GLA_DK) * (GLA_DK ** -0.5)
    gk = gk.reshape(B, L, GLA_HEADS, GLA_DK)
    gv = gv.reshape(B, L, GLA_HEADS, GLA_DV)
    log_a = jax.nn.log_sigmoid((ga @ w_gate_up[l] + b_gate[l]).astype(jnp.float32)) / GLA_TAU
    log_a = log_a.reshape(B, L, GLA_HEADS, GLA_DK)
    o_gla, S_new = gla_chunked(gq, gk, gv, log_a, S0)
    o_gla = rms_norm(o_gla, gla_norm[l]).reshape(B, L, GLA_HEADS * GLA_DV) * jax.nn.silu(gr)
    u_gla = o_gla @ w_gla_br[l]

    g_mla, g_gla = jnp.split(jax.nn.sigmoid(br), 2, axis=-1)
    out = (g_mla * u_mla + g_gla * u_gla) @ w_out[l]
    return out, ckv_new, kpe_new, S_new


def run_trunk(x, c, q_pos, gather_past, S0_all, w_ada, b_ada, w_in, q_norm, kv_norm, w_uq, w_uk, w_uv,
              w_mla_br, w_gate_up, b_gate, gla_norm, w_gla_br, w_out, ln1_g, ln1_b, w_up, w_down, ln2_g, ln2_b):
    ckv_rows, kpe_rows, states = [], [], []
    for l in range(DEPTH):
        mod = (jax.nn.silu(c) @ w_ada[l] + b_ada[l])[:, None, :]
        sh1, sc1, gt1, sh2, sc2, gt2 = jnp.split(mod, N_ADA, axis=-1)
        h = layer_norm(x) * (1 + sc1) + sh1
        mix, ckv_new, kpe_new, S_new = token_mixer(
            h, q_pos, gather_past(l), S0_all[l], l, w_in, q_norm, kv_norm, w_uq, w_uk, w_uv, w_mla_br,
            w_gate_up, b_gate, gla_norm, w_gla_br, w_out)
        x = layer_norm(DEEPNORM_ALPHA * x + (1 + gt1) * mix, ln1_g[l], ln1_b[l])
        h = layer_norm(x) * (1 + sc2) + sh2
        f = jnp.square(jax.nn.relu(h @ w_up[l])) @ w_down[l]
        x = layer_norm(DEEPNORM_ALPHA * x + (1 + gt2) * f, ln2_g[l], ln2_b[l])
        ckv_rows.append(ckv_new)
        kpe_rows.append(kpe_new)
        states.append(S_new)
    return x, jnp.stack(ckv_rows), jnp.stack(kpe_rows), jnp.stack(states)


def setup_inputs(seed: int = 0) -> dict:
    key = jax.random.key(seed)
    keys = jax.random.split(key, 40)
    it = iter(range(40))

    def nrm(shape, scale):
        return jax.random.normal(keys[next(it)], shape, jnp.float32) * scale

    n_pages = PAST_LEN // PAGE_SIZE
    n_used = DEC_BATCH * n_pages
    n_phys = n_used + max(1, n_used // 4)
    page_table = jax.random.permutation(keys[next(it)], n_phys)[:n_used].reshape(DEC_BATCH, n_pages).astype(jnp.int32)
    hqk = MLA_HEADS * (NOPE_DIM + ROPE_DIM)
    return {
        "x_prompt": nrm((BATCH, SEQ, D_MODEL), 1.0),
        "x_sample": nrm((DEC_BATCH, DEC_SEQ, D_MODEL), 1.0),
        "cache_ckv": nrm((DEPTH, n_phys, PAGE_SIZE, KV_LORA), 1.0),
        "cache_kpe": nrm((DEPTH, n_phys, PAGE_SIZE, ROPE_DIM), 1.0),
        "state_gla": nrm((DEPTH, DEC_BATCH, GLA_HEADS, GLA_DK, GLA_DV), 1.0),
        "page_table": page_table,
        "c_prompt": nrm((BATCH, D_MODEL), 1.0),
        "c_sample": nrm((DEC_BATCH, D_MODEL), 1.0),
        "w_ada": nrm((DEPTH, D_MODEL, N_ADA * D_MODEL), 0.1 * D_MODEL ** -0.5),
        "b_ada": nrm((DEPTH, N_ADA * D_MODEL), 0.02),
        "w_in": nrm((DEPTH, D_MODEL, D_IN), D_MODEL ** -0.5),
        "q_norm": 1.0 + nrm((DEPTH, Q_LORA), 0.02),
        "kv_norm": 1.0 + nrm((DEPTH, KV_LORA), 0.02),
        "w_uq": nrm((DEPTH, Q_LORA, hqk), Q_LORA ** -0.5),
        "w_uk": nrm((DEPTH, KV_LORA, MLA_HEADS, NOPE_DIM), KV_LORA ** -0.5),
        "w_uv": nrm((DEPTH, KV_LORA, MLA_HEADS, V_DIM), KV_LORA ** -0.5),
        "w_mla_br": nrm((DEPTH, MLA_HEADS * V_DIM, D_MODEL), (MLA_HEADS * V_DIM) ** -0.5),
        "w_gate_up": nrm((DEPTH, GLA_LOWRANK, GLA_HEADS * GLA_DK), GLA_LOWRANK ** -0.5),
        "b_gate": nrm((DEPTH, GLA_HEADS * GLA_DK), 0.02),
        "gla_norm": 1.0 + nrm((DEPTH, GLA_DV), 0.02),
        "w_gla_br": nrm((DEPTH, GLA_HEADS * GLA_DV, D_MODEL), (GLA_HEADS * GLA_DV) ** -0.5),
        "w_out": nrm((DEPTH, D_MODEL, D_MODEL), DEEPNORM_BETA * D_MODEL ** -0.5),
        "ln1_g": 1.0 + nrm((DEPTH, D_MODEL), 0.02),
        "ln1_b": nrm((DEPTH, D_MODEL), 0.02),
        "w_up": nrm((DEPTH, D_MODEL, D_FF), D_MODEL ** -0.5),
        "w_down": nrm((DEPTH, D_FF, D_MODEL), DEEPNORM_BETA * D_FF ** -0.5),
        "ln2_g": 1.0 + nrm((DEPTH, D_MODEL), 0.02),
        "ln2_b": nrm((DEPTH, D_MODEL), 0.02),
    }


def reference(x_prompt, x_sample, cache_ckv, cache_kpe, state_gla, page_table, c_prompt, c_sample,
              w_ada, b_ada, w_in, q_norm, kv_norm, w_uq, w_uk, w_uv, w_mla_br, w_gate_up, b_gate,
              gla_norm, w_gla_br, w_out, ln1_g, ln1_b, w_up, w_down, ln2_g, ln2_b):
    Bp, Lp, _ = x_prompt.shape
    pos_p = jnp.arange(Lp, dtype=jnp.int32)
    S0_p = jnp.zeros((DEPTH, Bp, GLA_HEADS, GLA_DK, GLA_DV), x_prompt.dtype)
    y_prompt, ckv_prompt, kpe_prompt, gla_prompt = run_trunk(
        x_prompt, c_prompt, pos_p, lambda l: None, S0_p,
        w_ada, b_ada, w_in, q_norm, kv_norm, w_uq, w_uk, w_uv, w_mla_br, w_gate_up, b_gate,
        gla_norm, w_gla_br, w_out, ln1_g, ln1_b, w_up, w_down, ln2_g, ln2_b)

    Bd, n_pages = page_table.shape
    Ls = x_sample.shape[1]
    past_len = n_pages * cache_ckv.shape[2]
    pos_s = past_len + jnp.arange(Ls, dtype=jnp.int32)

    def gather_past(l):
        ckv = cache_ckv[l][page_table].reshape(Bd, past_len, KV_LORA)
        kpe = cache_kpe[l][page_table].reshape(Bd, past_len, ROPE_DIM)
        return (ckv, kpe)

    y_sample, ckv_sample, kpe_sample, gla_sample = run_trunk(
        x_sample, c_sample, pos_s, gather_past, state_gla,
        w_ada, b_ada, w_in, q_norm, kv_norm, w_uq, w_uk, w_uv, w_mla_br, w_gate_up, b_gate,
        gla_norm, w_gla_br, w_out, ln1_g, ln1_b, w_up, w_down, ln2_g, ln2_b)

    return (y_prompt, y_sample, ckv_prompt, kpe_prompt, gla_prompt, ckv_sample, kpe_sample, gla_sample)
```

```python
import functools

import jax
import jax.numpy as jnp
from jax import lax
from jax.experimental import pallas as pl
from jax.experimental.pallas import tpu as pltpu

F32 = jnp.float32
BF16 = jnp.bfloat16

LANES = 128
SUBLANES = 8
ROPE_THETA = 10000.0
GLA_TAU = 16.0
GLA_CHUNK = 64
EPS = 1e-6
N_ADA = 6
ROW_TILE = 512
ATTN_TILE = 512
PAGES_PER_STEP = 16
VMEM_LIMIT = 56 * 1024 * 1024


def _cparams(sem):
    return pltpu.CompilerParams(dimension_semantics=sem, vmem_limit_bytes=VMEM_LIMIT)


def _resident(shape):
    nd = len(shape)
    return pl.BlockSpec(shape, lambda *_: (0,) * nd, pipeline_mode=pl.Buffered(1))


def _dot(a, b):
    return jnp.dot(a, b, preferred_element_type=F32)


def _dot_nt(a, b):
    return lax.dot_general(a, b, (((1,), (1,)), ((), ())), preferred_element_type=F32)


def _ln(x):
    mu = jnp.mean(x, axis=-1, keepdims=True)
    xc = x - mu
    var = jnp.mean(xc * xc, axis=-1, keepdims=True)
    return xc * lax.rsqrt(var + EPS)


def _rms(x):
    return x * lax.rsqrt(jnp.mean(x * x, axis=-1, keepdims=True) + EPS)


def _sigmoid(x):
    return 1.0 / (1.0 + jnp.exp(-x))


def _log_sigmoid(x):
    return jnp.minimum(x, 0.0) - jnp.log(1.0 + jnp.exp(-jnp.abs(x)))


def _ada_body(c_ref, w_ref, b_ref, o_ref):
    c = c_ref[...]
    s = (c * _sigmoid(c)).astype(BF16)
    o_ref[0] = _dot(s, w_ref[0].astype(BF16)) + b_ref[0]


def _ada(c_all, w_ada, b_ada):
    depth, d, n = w_ada.shape
    rows = c_all.shape[0]
    tn = 1536
    return pl.pallas_call(
        _ada_body,
        grid=(depth, n // tn),
        in_specs=[
            pl.BlockSpec((rows, d), lambda l, j: (0, 0)),
            pl.BlockSpec((1, d, tn), lambda l, j: (l, 0, j)),
            pl.BlockSpec((1, 1, tn), lambda l, j: (l, 0, j)),
        ],
        out_specs=pl.BlockSpec((1, rows, tn), lambda l, j: (l, 0, j)),
        out_shape=jax.ShapeDtypeStruct((depth, rows, n), F32),
        compiler_params=_cparams(("arbitrary", "arbitrary")),
        name="ada_mod",
    )(c_all, w_ada, b_ada.reshape(depth, 1, n))


def _token_tiles(b, l):
    if l >= ROW_TILE:
        assert l % ROW_TILE == 0
        return 1, ROW_TILE
    bt = min(b, ROW_TILE // l)
    assert b % bt == 0
    return bt, l


def _x_spec(bt, lt, d):
    return pl.BlockSpec((bt, lt, d), lambda b, l: (b, l, 0))


def _mod_spec(bt, d, k):
    return pl.BlockSpec((bt, 1, d), lambda b, l: (b, 0, k))


def _rows_spec(bt, lt, nl, n, col=0):
    return pl.BlockSpec((bt * lt, n), lambda b, l: (b * nl + l, col))


def _inproj_body(x_ref, sc_ref, sh_ref, w_ref, small_ref, big_ref, *, n_small, chunk):
    bt, lt, d = x_ref.shape
    h = _ln(x_ref[...]) * (1.0 + sc_ref[...]) + sh_ref[...]
    hb = h.reshape(bt * lt, d).astype(BF16)
    for c0 in range(0, w_ref.shape[1], chunk):
        acc = _dot(hb, w_ref[:, c0:c0 + chunk])
        if c0 < n_small:
            small_ref[:, c0:c0 + chunk] = acc
        else:
            big_ref[:, c0 - n_small:c0 - n_small + chunk] = acc.astype(BF16)


def _inproj(x, mod, w_in_r, n_small):
    b, l, d = x.shape
    bt, lt = _token_tiles(b, l)
    nl = l // lt
    n_big = w_in_r.shape[1] - n_small
    return pl.pallas_call(
        functools.partial(_inproj_body, n_small=n_small, chunk=512),
        grid=(b // bt, nl),
        in_specs=[_x_spec(bt, lt, d), _mod_spec(bt, d, 1), _mod_spec(bt, d, 0), _resident(w_in_r.shape)],
        out_specs=[_rows_spec(bt, lt, nl, n_small), _rows_spec(bt, lt, nl, n_big)],
        out_shape=[jax.ShapeDtypeStruct((b * l, n_small), F32), jax.ShapeDtypeStruct((b * l, n_big), BF16)],
        compiler_params=_cparams(("parallel", "parallel")),
        name="in_proj",
    )(x, mod, mod, w_in_r)


def _prep_common(small_ref, cos_ref, sin_ref, qn_ref, wuq_ref, kvn_ref, wg_ref, bg_ref,
                 ckv_ref, kpe_ref, la_ref, *, q_lora, kv_lora, rope):
    rows = small_ref.shape[0]
    lt = cos_ref.shape[0]
    s = small_ref[...]
    o = q_lora + kv_lora
    q_dn, kv_dn = s[:, :q_lora], s[:, q_lora:o]
    kr, krr, ga = s[:, o:o + LANES], s[:, o + LANES:o + 2 * LANES], s[:, o + 2 * LANES:o + 3 * LANES]
    cos, sin = cos_ref[...], sin_ref[...]
    if rows != lt:
        cos = jnp.broadcast_to(cos[None], (rows // lt, lt, LANES)).reshape(rows, LANES)
        sin = jnp.broadcast_to(sin[None], (rows // lt, lt, LANES)).reshape(rows, LANES)
    qn = (_rms(q_dn) * qn_ref[...]).astype(BF16)
    q2 = _dot(qn, wuq_ref[...])
    ckv = _rms(kv_dn) * kvn_ref[...]
    ckv_ref[...] = ckv
    kpe = kr * cos + krr * sin
    kpe_ref[...] = kpe[:, :rope]
    xg = _dot(ga.astype(BF16), wg_ref[...]) + bg_ref[...]
    la_ref[...] = _log_sigmoid(xg) * (1.0 / GLA_TAU)
    return q2, ckv, kpe, cos, sin


def _prep_prompt_body(small_ref, cos_ref, sin_ref, qn_ref, wuq_ref, kvn_ref, wg_ref, bg_ref, wuk_ref, wuv_ref,
                      ckv_ref, kpe_ref, la_ref, q_ref, k_ref, v_ref, *, heads, scale, **kw):
    q2, ckv, kpe, cos, sin = _prep_common(small_ref, cos_ref, sin_ref, qn_ref, wuq_ref, kvn_ref, wg_ref, bg_ref,
                                          ckv_ref, kpe_ref, la_ref, **kw)
    cb = ckv.astype(BF16)
    kn = _dot(cb, wuk_ref[...])
    v_ref[...] = _dot(cb, wuv_ref[...]).astype(BF16)
    hw = heads * LANES
    for h in range(heads):
        c0 = h * LANES
        qh = (q2[:, c0:c0 + LANES] * cos + q2[:, hw + c0:hw + c0 + LANES] * sin) * scale
        q_ref[:, c0:c0 + LANES] = qh.astype(BF16)
        k_ref[:, c0:c0 + LANES] = (kn[:, c0:c0 + LANES] + kpe).astype(BF16)


def _prep_sample_body(small_ref, cos_ref, sin_ref, qn_ref, wuq_ref, kvn_ref, wg_ref, bg_ref, wukt_ref,
                      ckv_ref, kpe_ref, la_ref, qlat_ref, qpe_ref, *, heads, scale, rope, **kw):
    q2, _, _, cos, sin = _prep_common(small_ref, cos_ref, sin_ref, qn_ref, wuq_ref, kvn_ref, wg_ref, bg_ref,
                                      ckv_ref, kpe_ref, la_ref, rope=rope, **kw)
    bt, _, lt, c_lat = qlat_ref.shape
    hw = heads * LANES
    for h in range(heads):
        c0 = h * LANES
        qh = (q2[:, c0:c0 + LANES] * cos + q2[:, hw + c0:hw + c0 + LANES] * sin) * scale
        qlat = _dot(qh.astype(BF16), wukt_ref[h])
        qlat_ref[:, h] = qlat.reshape(bt, lt, c_lat)
        qpe_ref[:, h] = qh[:, :rope].reshape(bt, lt, rope)


def _prep(small, cos_t, sin_t, b, l, wts, dims, sample):
    heads, q_lora, kv_lora, rope, nope = dims
    bt, lt = _token_tiles(b, l)
    nl = l // lt
    rows = bt * lt
    t = b * l
    scale = float(nope + rope) ** -0.5
    tab = pl.BlockSpec((lt, LANES), lambda bi, li: (li, 0))
    common_w = [wts["q_norm"], wts["w_uq2"], wts["kv_norm"], wts["w_gate"], wts["b_gate"]]
    common_out_specs = [_rows_spec(bt, lt, nl, kv_lora), _rows_spec(bt, lt, nl, rope),
                        _rows_spec(bt, lt, nl, wts["w_gate"].shape[1])]
    common_out_shape = [jax.ShapeDtypeStruct((t, kv_lora), F32), jax.ShapeDtypeStruct((t, rope), F32),
                        jax.ShapeDtypeStruct((t, wts["w_gate"].shape[1]), F32)]
    kw = dict(heads=heads, scale=scale, q_lora=q_lora, kv_lora=kv_lora, rope=rope)
    if not sample:
        extra_w = [wts["w_uk_p"], wts["w_uv"]]
        hw = heads * LANES
        nv = wts["w_uv"].shape[1]
        out_specs = common_out_specs + [_rows_spec(bt, lt, nl, hw), _rows_spec(bt, lt, nl, hw),
                                        _rows_spec(bt, lt, nl, nv)]
        out_shape = common_out_shape + [jax.ShapeDtypeStruct((t, hw), BF16), jax.ShapeDtypeStruct((t, hw), BF16),
                                        jax.ShapeDtypeStruct((t, nv), BF16)]
        body = functools.partial(_prep_prompt_body, **kw)
    else:
        extra_w = [wts["w_ukt_p"]]
        out_specs = common_out_specs + [
            pl.BlockSpec((bt, heads, lt, kv_lora), lambda bi, li: (bi, 0, li, 0)),
            pl.BlockSpec((bt, heads, lt, rope), lambda bi, li: (bi, 0, li, 0))]
        out_shape = common_out_shape + [jax.ShapeDtypeStruct((b, heads, l, kv_lora), F32),
                                        jax.ShapeDtypeStruct((b, heads, l, rope), F32)]
        body = functools.partial(_prep_sample_body, **kw)
    weights = common_w + extra_w
    return pl.pallas_call(
        body,
        grid=(b // bt, nl),
        in_specs=[_rows_spec(bt, lt, nl, small.shape[1]), tab, tab] + [_resident(w.shape) for w in weights],
        out_specs=out_specs,
        out_shape=out_shape,
        compiler_params=_cparams(("parallel", "parallel")),
        name="prep_sample" if sample else "prep_prompt",
    )(small, cos_t, sin_t, *weights)


def _flash_body(q_ref, k_ref, v_ref, o_ref, m_ref, l_ref, acc_ref, *, heads):
    i, j = pl.program_id(1), pl.program_id(2)
    tq, tk = q_ref.shape[1], k_ref.shape[1]
    rep = tk // LANES

    @pl.when(j == 0)
    def _():
        m_ref[...] = jnp.full(m_ref.shape, -jnp.inf, F32)
        l_ref[...] = jnp.zeros(l_ref.shape, F32)
        acc_ref[...] = jnp.zeros(acc_ref.shape, F32)

    def step(masked):
        if masked:
            keep = lax.broadcasted_iota(jnp.int32, (tq, tk), 0) >= lax.broadcasted_iota(jnp.int32, (tq, tk), 1)
        for h in range(heads):
            c0 = h * LANES
            s = _dot_nt(q_ref[0, :, c0:c0 + LANES], k_ref[0, :, c0:c0 + LANES])
            if masked:
                s = jnp.where(keep, s, -jnp.inf)
            m_prev = m_ref[h]
            m_new = jnp.maximum(m_prev, jnp.max(s, axis=-1, keepdims=True))
            p = jnp.exp(s - jnp.tile(m_new, (1, rep)))
            alpha = jnp.exp(m_prev - m_new)
            l_ref[h] = alpha * l_ref[h] + jnp.sum(p, axis=-1, keepdims=True)
            v0 = (h // 2) * LANES
            acc_ref[h] = alpha * acc_ref[h] + _dot(p.astype(BF16), v_ref[0, :, v0:v0 + LANES])
            m_ref[h] = m_new

    @pl.when(j < i)
    def _():
        step(False)

    @pl.when(j == i)
    def _():
        step(True)
        low = lax.broadcasted_iota(jnp.int32, (tq, LANES), 1) < LANES // 2
        for hp in range(heads // 2):
            a = acc_ref[2 * hp] / l_ref[2 * hp]
            b = acc_ref[2 * hp + 1] / l_ref[2 * hp + 1]
            o_ref[0, :, hp * LANES:(hp + 1) * LANES] = jnp.where(low, a, b).astype(BF16)


def _flash(q, k, v, b, l, heads):
    t = min(ATTN_TILE, l)
    n = l // t
    hw = heads * LANES
    nv = v.shape[1]
    q3, k3, v3 = q.reshape(b, l, hw), k.reshape(b, l, hw), v.reshape(b, l, nv)
    out = pl.pallas_call(
        functools.partial(_flash_body, heads=heads),
        grid=(b, n, n),
        in_specs=[
            pl.BlockSpec((1, t, hw), lambda bi, i, j: (bi, i, 0)),
            pl.BlockSpec((1, t, hw), lambda bi, i, j: (bi, jnp.minimum(i, j), 0)),
            pl.BlockSpec((1, t, nv), lambda bi, i, j: (bi, jnp.minimum(i, j), 0)),
        ],
        out_specs=pl.BlockSpec((1, t, nv), lambda bi, i, j: (bi, i, 0)),
        out_shape=jax.ShapeDtypeStruct((b, l, nv), BF16),
        scratch_shapes=[pltpu.VMEM((heads, t, LANES), F32), pltpu.VMEM((heads, t, LANES), F32),
                        pltpu.VMEM((heads, t, LANES), F32)],
        compiler_params=_cparams(("parallel", "parallel", "arbitrary")),
        name="flash_prompt",
    )(q3, k3, v3)
    return out.reshape(b * l, nv)


def _decode_body(pt_ref, qlat_ref, qpe_ref, ckvn_ref, kpen_ref, *rest, n_step, page):
    ckv_refs, kpe_refs = rest[:n_step], rest[n_step:2 * n_step]
    o_ref, m_ref, l_ref, acc_ref, kb_ref, pb_ref, kn_ref, pn_ref = rest[2 * n_step:]
    c = pl.program_id(1)
    _, heads, lq, c_lat = qlat_ref.shape
    rope = qpe_ref.shape[3]
    rows = heads * lq
    ql = qlat_ref[0].reshape(rows, c_lat).astype(BF16)
    qp = qpe_ref[0].reshape(rows, rope).astype(BF16)

    def update(s, vals, first):
        m_cur = jnp.max(s, axis=-1, keepdims=True)
        if first:
            m_new = jnp.broadcast_to(m_cur, (rows, LANES))
            p = jnp.exp(s - jnp.tile(m_new, (1, s.shape[1] // LANES)))
            l_ref[...] = jnp.broadcast_to(jnp.sum(p, axis=-1, keepdims=True), (rows, LANES))
            acc_ref[...] = _dot(p.astype(BF16), vals)
        else:
            m_prev = m_ref[...]
            m_new = jnp.maximum(m_prev, m_cur)
            p = jnp.exp(s - jnp.tile(m_new, (1, s.shape[1] // LANES)))
            alpha = jnp.exp(m_prev - m_new)
            l_ref[...] = alpha * l_ref[...] + jnp.sum(p, axis=-1, keepdims=True)
            acc_ref[...] = jnp.tile(alpha, (1, c_lat // LANES)) * acc_ref[...] + _dot(p.astype(BF16), vals)
        m_ref[...] = m_new

    @pl.when(c == 0)
    def _():
        kn_ref[...] = jnp.zeros(kn_ref.shape, F32)
        pn_ref[...] = jnp.zeros(pn_ref.shape, F32)
        kn_ref[0:lq, :] = ckvn_ref[...]
        pn_ref[0:lq, :] = kpen_ref[...]
        kn = kn_ref[...].astype(BF16)
        s = _dot_nt(ql, kn) + _dot_nt(qp, pn_ref[...].astype(BF16))
        key = lax.broadcasted_iota(jnp.int32, (rows, LANES), 1)
        qpos = lax.broadcasted_iota(jnp.int32, (rows, LANES), 0) % lq
        s = jnp.where(key <= qpos, s, -jnp.inf)
        update(s, kn, True)

    for p in range(n_step):
        kb_ref[p * page:(p + 1) * page, :] = ckv_refs[p][...].astype(BF16)
        pb_ref[p * page:(p + 1) * page, :] = kpe_refs[p][...].astype(BF16)
    kb = kb_ref[...]
    s = _dot_nt(ql, kb) + _dot_nt(qp, pb_ref[...])
    update(s, kb, False)

    @pl.when(c == pl.num_programs(1) - 1)
    def _():
        o = acc_ref[...] / jnp.tile(l_ref[...], (1, c_lat // LANES))
        o_ref[0] = o.reshape(heads, lq, c_lat)


def _decode(qlat, qpe, ckv_new, kpe_new, cache_ckv, cache_kpe, page_table, layer):
    b, heads, lq, c_lat = qlat.shape
    rope = qpe.shape[3]
    n_pages = page_table.shape[1]
    page = cache_ckv.shape[2]
    n_step = min(PAGES_PER_STEP, n_pages)
    assert n_pages % n_step == 0 and lq % SUBLANES == 0 and lq <= LANES
    rows = heads * lq

    def page_map(p):
        return lambda bi, c, pt: (layer, pt[bi * n_pages + c * n_step + p], 0, 0)

    in_specs = [
        pl.BlockSpec((1, heads, lq, c_lat), lambda bi, c, pt: (bi, 0, 0, 0)),
        pl.BlockSpec((1, heads, lq, rope), lambda bi, c, pt: (bi, 0, 0, 0)),
        pl.BlockSpec((lq, c_lat), lambda bi, c, pt: (bi, 0)),
        pl.BlockSpec((lq, rope), lambda bi, c, pt: (bi, 0)),
    ]
    in_specs += [pl.BlockSpec((None, None, page, c_lat), page_map(p)) for p in range(n_step)]
    in_specs += [pl.BlockSpec((None, None, page, rope), page_map(p)) for p in range(n_step)]
    grid_spec = pltpu.PrefetchScalarGridSpec(
        num_scalar_prefetch=1,
        grid=(b, n_pages // n_step),
        in_specs=in_specs,
        out_specs=pl.BlockSpec((1, heads, lq, c_lat), lambda bi, c, pt: (bi, 0, 0, 0)),
        scratch_shapes=[pltpu.VMEM((rows, LANES), F32), pltpu.VMEM((rows, LANES), F32),
                        pltpu.VMEM((rows, c_lat), F32),
                        pltpu.VMEM((n_step * page, c_lat), BF16), pltpu.VMEM((n_step * page, rope), BF16),
                        pltpu.VMEM((LANES, c_lat), F32), pltpu.VMEM((LANES, rope), F32)],
    )
    return pl.pallas_call(
        functools.partial(_decode_body, n_step=n_step, page=page),
        grid_spec=grid_spec,
        out_shape=jax.ShapeDtypeStruct((b, heads, lq, c_lat), F32),
        compiler_params=_cparams(("parallel", "arbitrary")),
        name="decode_sample",
    )(page_table.reshape(-1), qlat, qpe, ckv_new, kpe_new,
      *([cache_ckv] * n_step), *([cache_kpe] * n_step))


def _gla_chunk_intra(q, k, v, g, seg):
    c, dk = q.shape
    dv = v.shape[1]
    row = lax.broadcasted_iota(jnp.int32, (c, c), 0)
    col = lax.broadcasted_iota(jnp.int32, (c, c), 1)
    tri = ((row >= col) & (row // seg == col // seg)).astype(BF16)
    g_hi = g.astype(BF16)
    g_lo = (g - g_hi.astype(F32)).astype(BF16)
    b = _dot(tri, g_hi) + _dot(tri, g_lo)

    a = jnp.zeros((c, c), F32)
    m = seg // 2
    while m >= SUBLANES:
        b3 = b.reshape(c // (2 * m), 2 * m, dk)
        ref = jnp.broadcast_to(b3[:, m - 1:m, :], b3.shape).reshape(c, dk)
        f = jnp.exp(-jnp.abs(b - ref))
        pm = _dot_nt((q * f).astype(BF16), (k * f).astype(BF16))
        valid = (row // (2 * m) == col // (2 * m)) & (row % (2 * m) >= m) & (col % (2 * m) < m)
        a = a + jnp.where(valid, pm, 0.0)
        m //= 2
    o = _dot(a.astype(BF16), v.astype(BF16)) if seg > SUBLANES else jnp.zeros((c, dv), F32)

    nsub = c // SUBLANES
    q3, k3, b3 = (x.reshape(nsub, SUBLANES, dk) for x in (q, k, b))
    v3 = v.reshape(nsub, SUBLANES, dv)
    trow = lax.broadcasted_iota(jnp.int32, (nsub, SUBLANES, 1), 1)
    od = jnp.zeros((nsub, SUBLANES, dv), F32)
    for s in range(SUBLANES):
        bs = jnp.broadcast_to(b3[:, s:s + 1, :], b3.shape)
        ks = jnp.broadcast_to(k3[:, s:s + 1, :], k3.shape)
        w = q3 * ks * jnp.exp(jnp.minimum(b3 - bs, 0.0))
        d = jnp.where(trow >= s, jnp.sum(w, axis=-1, keepdims=True), 0.0)
        od = od + d * jnp.broadcast_to(v3[:, s:s + 1, :], v3.shape)
    return b, o + od.reshape(c, dv)


def _gla_body(*refs, nb, tl, seg, use_s0, scale):
    if use_s0:
        q_ref, k_ref, v_ref, g_ref, gn_ref, s0_ref, o_ref, sf_ref, st_ref = refs
    else:
        q_ref, k_ref, v_ref, g_ref, gn_ref, o_ref, sf_ref, st_ref = refs
    li = pl.program_id(2)
    c = GLA_CHUNK
    dk, dv = q_ref.shape[1], v_ref.shape[1]

    @pl.when(li == 0)
    def _():
        for ib in range(nb):
            st_ref[ib] = s0_ref[ib, 0].T if use_s0 else jnp.zeros((dv, dk), F32)

    gn = gn_ref[...]
    n_chunks = nb * tl // c
    nseg = c // seg
    rowseg = lax.broadcasted_iota(jnp.int32, (c, 1), 0) // seg
    for ci in range(n_chunks):
        r0 = ci * c
        q = q_ref[r0:r0 + c, :].astype(F32) * scale
        k = k_ref[r0:r0 + c, :].astype(F32)
        v = v_ref[r0:r0 + c, :].astype(F32)
        g = g_ref[r0:r0 + c, :]
        b, o = _gla_chunk_intra(q, k, v, g, seg)
        b3 = b.reshape(nseg, seg, dk)
        b_last = jnp.broadcast_to(b3[:, seg - 1:seg, :], b3.shape).reshape(c, dk)
        qe = (q * jnp.exp(b)).astype(BF16)
        ke = (k * jnp.exp(b_last - b)).astype(BF16)
        vt = v.T.astype(BF16)
        for sg in range(nseg):
            ib = (ci * nseg + sg) if nseg > 1 else (ci * c) // tl
            st = st_ref[ib]
            if nseg > 1:
                own = rowseg == sg
                o = o + jnp.where(own, _dot_nt(qe, st.astype(BF16)), 0.0)
                ke_s = jnp.where(own, ke, jnp.zeros_like(ke))
            else:
                o = o + _dot_nt(qe, st.astype(BF16))
                ke_s = ke
            dec = jnp.exp(b[(sg + 1) * seg - 1:(sg + 1) * seg, :])
            st_ref[ib] = st * dec + _dot(vt, ke_s)
        o_ref[r0:r0 + c, :] = (_rms(o) * gn).astype(BF16)

    @pl.when(li == pl.num_programs(2) - 1)
    def _():
        for ib in range(nb):
            sf_ref[ib, 0] = st_ref[ib].T


def _gla(big, log_a, gla_norm, s0, b, l, heads, dk, dv, cols):
    c = GLA_CHUNK
    if l >= c:
        assert l % c == 0
        nb, seg = 1, c
        tl = min(ROW_TILE, l)
    else:
        assert c % l == 0 and l % SUBLANES == 0
        nb, seg, tl = c // l, l, l
        assert b % nb == 0
    nl = l // tl
    rows = nb * tl
    q0, k0, v0 = cols

    def tok(width, base):
        return pl.BlockSpec((rows, width), lambda bi, h, li: (bi * nl + li, base // width + h))

    in_specs = [tok(dk, q0), tok(dk, k0), tok(dv, v0), tok(dk, 0), pl.BlockSpec((1, dv), lambda bi, h, li: (0, 0))]
    args = [big, big, big, log_a, gla_norm]
    state_spec = pl.BlockSpec((nb, 1, dk, dv), lambda bi, h, li: (bi, h, 0, 0))
    if s0 is not None:
        in_specs.append(state_spec)
        args.append(s0)
    return pl.pallas_call(
        functools.partial(_gla_body, nb=nb, tl=tl, seg=seg, use_s0=s0 is not None, scale=float(dk) ** -0.5),
        grid=(b // nb, heads, nl),
        in_specs=in_specs,
        out_specs=[pl.BlockSpec((rows, dv), lambda bi, h, li: (bi * nl + li, h)), state_spec],
        out_shape=[jax.ShapeDtypeStruct((b * l, heads * dv), BF16), jax.ShapeDtypeStruct((b, heads, dk, dv), F32)],
        scratch_shapes=[pltpu.VMEM((nb, dv, dk), F32)],
        compiler_params=_cparams(("parallel", "parallel", "arbitrary")),
        name="gla",
    )(*args)


def _post_body(*refs, sample, alpha):
    if sample:
        (x_ref, gt_ref, olat_ref, ogla_ref, gr_ref, bra_ref, brb_ref, wuv_ref,
         wmla_ref, wgla_ref, wout_ref, g_ref, b_ref, o_ref) = refs
        bt, heads, lt, c_lat = olat_ref.shape
        o_mla = _dot(olat_ref[:, 0].reshape(bt * lt, c_lat).astype(BF16), wuv_ref[0])
        for h in range(1, heads):
            o_mla = o_mla + _dot(olat_ref[:, h].reshape(bt * lt, c_lat).astype(BF16), wuv_ref[h])
        o_mla = o_mla.astype(BF16)
    else:
        (x_ref, gt_ref, omla_ref, ogla_ref, gr_ref, bra_ref, brb_ref,
         wmla_ref, wgla_ref, wout_ref, g_ref, b_ref, o_ref) = refs
        o_mla = omla_ref[...]
    bt, lt, d = x_ref.shape
    u_mla = _dot(o_mla, wmla_ref[...])
    gr = gr_ref[...].astype(F32)
    og = (ogla_ref[...].astype(F32) * (gr * _sigmoid(gr))).astype(BF16)
    u_gla = _dot(og, wgla_ref[...])
    merged = _sigmoid(bra_ref[...].astype(F32)) * u_mla + _sigmoid(brb_ref[...].astype(F32)) * u_gla
    mix = _dot(merged.astype(BF16), wout_ref[...]).reshape(bt, lt, d)
    y = alpha * x_ref[...] + (1.0 + gt_ref[...]) * mix
    o_ref[...] = _ln(y) * g_ref[...] + b_ref[...]


def _post(x, mod, o_mla, o_gla, big, wts, cols, alpha, sample):
    b, l, d = x.shape
    bt, lt = _token_tiles(b, l)
    nl = l // lt
    gr0, br0 = cols
    if sample:
        heads, c_lat = o_mla.shape[1], o_mla.shape[3]
        mla_spec = pl.BlockSpec((bt, heads, lt, c_lat), lambda bi, li: (bi, 0, li, 0))
        weights = [wts["w_uv_p"], wts["w_mla_br"], wts["w_gla_br"], wts["w_out"], wts["ln1_g"], wts["ln1_b"]]
    else:
        mla_spec = _rows_spec(bt, lt, nl, o_mla.shape[1])
        weights = [wts["w_mla_br"], wts["w_gla_br"], wts["w_out"], wts["ln1_g"], wts["ln1_b"]]
    in_specs = [_x_spec(bt, lt, d), _mod_spec(bt, d, 2), mla_spec, _rows_spec(bt, lt, nl, o_gla.shape[1]),
                _rows_spec(bt, lt, nl, d, gr0 // d), _rows_spec(bt, lt, nl, d, br0 // d),
                _rows_spec(bt, lt, nl, d, br0 // d + 1)]
    return pl.pallas_call(
        functools.partial(_post_body, sample=sample, alpha=alpha),
        grid=(b // bt, nl),
        in_specs=in_specs + [_resident(w.shape) for w in weights],
        out_specs=_x_spec(bt, lt, d),
        out_shape=jax.ShapeDtypeStruct((b, l, d), F32),
        compiler_params=_cparams(("parallel", "parallel")),
        name="merge_sample" if sample else "merge_prompt",
    )(x, mod, o_mla, o_gla, big, big, big, *weights)


def _mlp_body(x_ref, sc_ref, sh_ref, gt_ref, wup_ref, wdn_ref, g_ref, b_ref, o_ref, *, alpha, chunk):
    bt, lt, d = x_ref.shape
    x = x_ref[...]
    hb = (_ln(x) * (1.0 + sc_ref[...]) + sh_ref[...]).reshape(bt * lt, d).astype(BF16)
    f = jnp.zeros((bt * lt, d), F32)
    for c0 in range(0, wup_ref.shape[1], chunk):
        u = jnp.maximum(_dot(hb, wup_ref[:, c0:c0 + chunk]), 0.0)
        f = f + _dot((u * u).astype(BF16), wdn_ref[c0:c0 + chunk, :])
    y = alpha * x + (1.0 + gt_ref[...]) * f.reshape(bt, lt, d)
    o_ref[...] = _ln(y) * g_ref[...] + b_ref[...]


def _mlp(x, mod, wts, alpha):
    b, l, d = x.shape
    bt, lt = _token_tiles(b, l)
    weights = [wts["w_up"], wts["w_down"], wts["ln2_g"], wts["ln2_b"]]
    return pl.pallas_call(
        functools.partial(_mlp_body, alpha=alpha, chunk=1024),
        grid=(b // bt, l // lt),
        in_specs=[_x_spec(bt, lt, d), _mod_spec(bt, d, 4), _mod_spec(bt, d, 3), _mod_spec(bt, d, 5)]
        + [_resident(w.shape) for w in weights],
        out_specs=_x_spec(bt, lt, d),
        out_shape=jax.ShapeDtypeStruct((b, l, d), F32),
        compiler_params=_cparams(("parallel", "parallel")),
        name="mlp",
    )(x, mod, mod, mod, *weights)


def _rot_cols(w):
    half = w.shape[-1] // 2
    return jnp.concatenate([-w[..., half:], w[..., :half]], axis=-1)


def _pad_cols(w, width):
    return jnp.pad(w, [(0, 0)] * (w.ndim - 1) + [(0, width - w.shape[-1])])


def _rope_tables(pos, rope):
    half = rope // 2
    inv = ROPE_THETA ** (-jnp.arange(half, dtype=F32) / half)
    ang = pos.astype(F32)[:, None] * inv[None, :]
    cos, sin = jnp.cos(ang), jnp.sin(ang)
    n = pos.shape[0]
    cos_t = jnp.concatenate([cos, cos, jnp.ones((n, LANES - 2 * rope), F32), jnp.zeros((n, rope), F32)], axis=1)
    sin_t = jnp.concatenate([sin, sin, jnp.zeros((n, LANES - rope), F32)], axis=1)
    return cos_t, sin_t


def kernel(x_prompt, x_sample, cache_ckv, cache_kpe, state_gla, page_table, c_prompt, c_sample, w_ada, b_ada, w_in, q_norm, kv_norm, w_uq, w_uk, w_uv, w_mla_br, w_gate_up, b_gate, gla_norm, w_gla_br, w_out, ln1_g, ln1_b, w_up, w_down, ln2_g, ln2_b):
    depth, d, _ = w_in.shape
    q_lora, kv_lora = q_norm.shape[1], kv_norm.shape[1]
    rope = cache_kpe.shape[3]
    heads, nope = w_uk.shape[2], w_uk.shape[3]
    v_dim = w_uv.shape[3]
    lowrank = w_gate_up.shape[1]
    dv = gla_norm.shape[1]
    gheads = w_gla_br.shape[1] // dv
    dk = w_gate_up.shape[2] // gheads
    alpha = (2.0 * depth) ** 0.25
    assert rope + nope <= LANES and 2 * v_dim == LANES and heads % 2 == 0 and lowrank <= LANES

    splits = (q_lora, kv_lora, rope, gheads * dk, gheads * dk, gheads * dv, lowrank, gheads * dv, 2 * d)
    offs = [0]
    for s in splits:
        offs.append(offs[-1] + s)
    seg = [w_in[:, :, offs[i]:offs[i + 1]] for i in range(len(splits))]
    w_qdn, w_kvdn, w_kr, w_gq, w_gk, w_gv, w_ga, w_gr, w_br = seg
    small_cols = [w_qdn, w_kvdn, _pad_cols(w_kr, LANES), _pad_cols(_rot_cols(w_kr), LANES), _pad_cols(w_ga, LANES)]
    n_small = sum(w.shape[2] for w in small_cols)
    assert n_small % 512 == 0 and q_lora % LANES == 0 and kv_lora % LANES == 0
    w_in_r = jnp.concatenate(small_cols + [w_gq, w_gk, w_gv, w_gr, w_br], axis=2).astype(BF16)
    q0, k0 = 0, gheads * dk
    v0 = 2 * gheads * dk
    gr0 = v0 + gheads * dv
    br0 = gr0 + gheads * dv

    w_uq_h = w_uq.reshape(depth, q_lora, heads, nope + rope)
    w_q_nope, w_q_rope = w_uq_h[..., :nope], w_uq_h[..., nope:]
    zpad = jnp.zeros((depth, q_lora, heads, LANES - rope - nope), F32)
    w_q_main = jnp.concatenate([w_q_rope, w_q_nope, zpad], axis=-1)
    w_q_rot = jnp.concatenate([_rot_cols(w_q_rope), jnp.zeros_like(w_q_nope), zpad], axis=-1)
    w_uq2 = jnp.concatenate([w_q_main.reshape(depth, q_lora, heads * LANES),
                             w_q_rot.reshape(depth, q_lora, heads * LANES)], axis=-1).astype(BF16)
    w_uk_p = jnp.pad(w_uk, ((0, 0), (0, 0), (0, 0), (rope, LANES - rope - nope)))
    w_ukt_p = jnp.transpose(w_uk_p, (0, 2, 3, 1)).astype(BF16)
    w_uk_p = w_uk_p.reshape(depth, kv_lora, heads * LANES).astype(BF16)
    w_uv_f = w_uv.reshape(depth, kv_lora, heads * v_dim).astype(BF16)
    eye = jnp.eye(heads, dtype=F32)
    w_uv_p = (w_uv.transpose(0, 2, 1, 3)[:, :, :, None, :] * eye[None, :, None, :, None]).reshape(
        depth, heads, kv_lora, heads * v_dim).astype(BF16)
    w_gate = jnp.pad(w_gate_up, ((0, 0), (0, LANES - lowrank), (0, 0))).astype(BF16)

    def layer_weights(l):
        return dict(
            q_norm=q_norm[l][None], kv_norm=kv_norm[l][None], w_uq2=w_uq2[l], w_uk_p=w_uk_p[l], w_ukt_p=w_ukt_p[l],
            w_uv=w_uv_f[l], w_uv_p=w_uv_p[l], w_gate=w_gate[l], b_gate=b_gate[l][None],
            w_mla_br=w_mla_br[l].astype(BF16), w_gla_br=w_gla_br[l].astype(BF16), w_out=w_out[l].astype(BF16),
            ln1_g=ln1_g[l][None], ln1_b=ln1_b[l][None], w_up=w_up[l].astype(BF16), w_down=w_down[l].astype(BF16),
            ln2_g=ln2_g[l][None], ln2_b=ln2_b[l][None])

    bp, lp, _ = x_prompt.shape
    bs, ls, _ = x_sample.shape
    past_len = page_table.shape[1] * cache_ckv.shape[2]
    mod_all = _ada(jnp.concatenate([c_prompt, c_sample], axis=0), w_ada, b_ada)
    tabs_p = _rope_tables(jnp.arange(lp, dtype=jnp.int32), rope)
    tabs_s = _rope_tables(past_len + jnp.arange(ls, dtype=jnp.int32), rope)
    dims = (heads, q_lora, kv_lora, rope, nope)

    xp, xs = x_prompt, x_sample
    outs = {k: [] for k in ("ckv_p", "kpe_p", "gla_p", "ckv_s", "kpe_s", "gla_s")}
    for l in range(depth):
        wts = layer_weights(l)
        gn = gla_norm[l][None]
        mod = mod_all[l, :bp][:, None, :]
        small, big = _inproj(xp, mod, w_in_r[l], n_small)
        ckv, kpe, log_a, q, k, v = _prep(small, *tabs_p, bp, lp, wts, dims, sample=False)
        o_mla = _flash(q, k, v, bp, lp, heads)
        o_gla, s_new = _gla(big, log_a, gn, None, bp, lp, gheads, dk, dv, (q0, k0, v0))
        x1 = _post(xp, mod, o_mla, o_gla, big, wts, (gr0, br0), alpha, sample=False)
        xp = _mlp(x1, mod, wts, alpha)
        outs["ckv_p"].append(ckv.reshape(bp, lp, kv_lora))
        outs["kpe_p"].append(kpe.reshape(bp, lp, rope))
        outs["gla_p"].append(s_new)
        mod = mod_all[l, bp:][:, None, :]
        small, big = _inproj(xs, mod, w_in_r[l], n_small)
        ckv, kpe, log_a, qlat, qpe = _prep(small, *tabs_s, bs, ls, wts, dims, sample=True)
        o_lat = _decode(qlat, qpe, ckv, kpe, cache_ckv, cache_kpe, page_table, l)
        o_gla, s_new = _gla(big, log_a, gn, state_gla[l], bs, ls, gheads, dk, dv, (q0, k0, v0))
        x1 = _post(xs, mod, o_lat, o_gla, big, wts, (gr0, br0), alpha, sample=True)
        xs = _mlp(x1, mod, wts, alpha)
        outs["ckv_s"].append(ckv.reshape(bs, ls, kv_lora))
        outs["kpe_s"].append(kpe.reshape(bs, ls, rope))
        outs["gla_s"].append(s_new)

    st = {k: jnp.stack(v) for k, v in outs.items()}
    return (xp, xs, st["ckv_p"], st["kpe_p"], st["gla_p"], st["ckv_s"], st["kpe_s"], st["gla_s"])
```

```python
import functools

import jax
import jax.numpy as jnp
from jax import lax
from jax.experimental import pallas as pl
from jax.experimental.pallas import tpu as pltpu

F32 = jnp.float32
BF16 = jnp.bfloat16

LANES = 128
SUBLANES = 8
ROPE_THETA = 10000.0
GLA_TAU = 16.0
LOG2_E = 1.4426950408889634
GLA_CHUNK = 128
GLA_SHORT_CHUNK = 64
EPS = 1e-6
N_ADA = 6
ROW_TILE = 512
ATTN_TILE = 512
PAGES_PER_CHUNK = 32
VMEM_LIMIT = 56 * 1024 * 1024


def _cparams(sem):
    return pltpu.CompilerParams(dimension_semantics=sem, vmem_limit_bytes=VMEM_LIMIT)


def _resident(shape):
    nd = len(shape)
    return pl.BlockSpec(shape, lambda *_: (0,) * nd, pipeline_mode=pl.Buffered(1))


def _dot(a, b):
    return jnp.dot(a, b, preferred_element_type=F32)


def _dot_nt(a, b):
    return lax.dot_general(a, b, (((1,), (1,)), ((), ())), preferred_element_type=F32)


def _ln(x):
    mu = jnp.mean(x, axis=-1, keepdims=True)
    xc = x - mu
    var = jnp.mean(xc * xc, axis=-1, keepdims=True)
    return xc * lax.rsqrt(var + EPS)


def _rms(x):
    return x * lax.rsqrt(jnp.mean(x * x, axis=-1, keepdims=True) + EPS)


def _sigmoid(x):
    return 1.0 / (1.0 + jnp.exp(-x))


def _log_sigmoid(x):
    return jnp.minimum(x, 0.0) - jnp.log(1.0 + jnp.exp(-jnp.abs(x)))


def _ada_body(c_ref, w_ref, b_ref, o_ref):
    c = c_ref[...]
    s = (c * _sigmoid(c)).astype(BF16)
    o_ref[0] = _dot(s, w_ref[0].astype(BF16)) + b_ref[0]


def _ada(c_all, w_ada, b_ada):
    depth, d, n = w_ada.shape
    rows = c_all.shape[0]
    tn = 1536
    return pl.pallas_call(
        _ada_body,
        grid=(depth, n // tn),
        in_specs=[
            pl.BlockSpec((rows, d), lambda l, j: (0, 0)),
            pl.BlockSpec((1, d, tn), lambda l, j: (l, 0, j)),
            pl.BlockSpec((1, 1, tn), lambda l, j: (l, 0, j)),
        ],
        out_specs=pl.BlockSpec((1, rows, tn), lambda l, j: (l, 0, j)),
        out_shape=jax.ShapeDtypeStruct((depth, rows, n), F32),
        compiler_params=_cparams(("arbitrary", "arbitrary")),
        name="ada_mod",
    )(c_all, w_ada, b_ada.reshape(depth, 1, n))


def _token_tiles(b, l):
    if l >= ROW_TILE:
        assert l % ROW_TILE == 0
        return 1, ROW_TILE
    bt = min(b, ROW_TILE // l)
    assert b % bt == 0
    return bt, l


def _x_spec(bt, lt, d):
    return pl.BlockSpec((bt, lt, d), lambda b, l: (b, l, 0))


def _mod_spec(bt, d, k):
    return pl.BlockSpec((bt, 1, d), lambda b, l: (b, 0, k))


def _rows_spec(bt, lt, nl, n, col=0):
    return pl.BlockSpec((bt * lt, n), lambda b, l: (b * nl + l, col))


def _inproj_body(x_ref, sc_ref, sh_ref, w_ref, small_ref, big_ref, *, n_small, chunk):
    bt, lt, d = x_ref.shape
    h = _ln(x_ref[...]) * (1.0 + sc_ref[...]) + sh_ref[...]
    hb = h.reshape(bt * lt, d).astype(BF16)
    for c0 in range(0, w_ref.shape[1], chunk):
        acc = _dot(hb, w_ref[:, c0:c0 + chunk])
        if c0 < n_small:
            small_ref[:, c0:c0 + chunk] = acc
        else:
            big_ref[:, c0 - n_small:c0 - n_small + chunk] = acc.astype(BF16)


def _inproj(x, mod, w_in_r, n_small):
    b, l, d = x.shape
    bt, lt = _token_tiles(b, l)
    nl = l // lt
    n_big = w_in_r.shape[1] - n_small
    return pl.pallas_call(
        functools.partial(_inproj_body, n_small=n_small, chunk=512),
        grid=(b // bt, nl),
        in_specs=[_x_spec(bt, lt, d), _mod_spec(bt, d, 1), _mod_spec(bt, d, 0), _resident(w_in_r.shape)],
        out_specs=[_rows_spec(bt, lt, nl, n_small), _rows_spec(bt, lt, nl, n_big)],
        out_shape=[jax.ShapeDtypeStruct((b * l, n_small), F32), jax.ShapeDtypeStruct((b * l, n_big), BF16)],
        compiler_params=_cparams(("parallel", "parallel")),
        name="in_proj",
    )(x, mod, mod, w_in_r)


def _prep_common(small_ref, cos_ref, sin_ref, qn_ref, wuq_ref, kvn_ref, wg_ref, bg_ref,
                 ckv_ref, kpe_ref, la_ref, *, q_lora, kv_lora, rope):
    rows = small_ref.shape[0]
    lt = cos_ref.shape[0]
    s = small_ref[...]
    o = q_lora + kv_lora
    q_dn, kv_dn = s[:, :q_lora], s[:, q_lora:o]
    kr, krr, ga = s[:, o:o + LANES], s[:, o + LANES:o + 2 * LANES], s[:, o + 2 * LANES:o + 3 * LANES]
    cos, sin = cos_ref[...], sin_ref[...]
    if rows != lt:
        cos = jnp.broadcast_to(cos[None], (rows // lt, lt, LANES)).reshape(rows, LANES)
        sin = jnp.broadcast_to(sin[None], (rows // lt, lt, LANES)).reshape(rows, LANES)
    qn = (_rms(q_dn) * qn_ref[...]).astype(BF16)
    q2 = _dot(qn, wuq_ref[...])
    ckv = _rms(kv_dn) * kvn_ref[...]
    ckv_ref[...] = ckv
    kpe = kr * cos + krr * sin
    kpe_ref[...] = kpe[:, :rope]
    xg = _dot(ga.astype(BF16), wg_ref[...]) + bg_ref[...]
    la_ref[...] = _log_sigmoid(xg) * (1.0 / GLA_TAU)
    return q2, ckv, kpe, cos, sin


def _prep_prompt_body(small_ref, cos_ref, sin_ref, qn_ref, wuq_ref, kvn_ref, wg_ref, bg_ref, wuk_ref, wuv_ref,
                      ckv_ref, kpe_ref, la_ref, q_ref, k_ref, v_ref, *, heads, scale, **kw):
    q2, ckv, kpe, cos, sin = _prep_common(small_ref, cos_ref, sin_ref, qn_ref, wuq_ref, kvn_ref, wg_ref, bg_ref,
                                          ckv_ref, kpe_ref, la_ref, **kw)
    cb = ckv.astype(BF16)
    kn = _dot(cb, wuk_ref[...])
    hw = heads * LANES
    v = _dot(cb, wuv_ref[...])
    upper = lax.broadcasted_iota(jnp.int32, v.shape, 1) % LANES >= LANES // 2
    v_ref[...] = jnp.where(upper, 1.0, v).astype(BF16)
    for h in range(heads):
        c0 = h * LANES
        qh = (q2[:, c0:c0 + LANES] * cos + q2[:, hw + c0:hw + c0 + LANES] * sin) * scale
        q_ref[:, c0:c0 + LANES] = qh.astype(BF16)
        k_ref[:, c0:c0 + LANES] = (kn[:, c0:c0 + LANES] + kpe).astype(BF16)


def _prep_sample_body(small_ref, cos_ref, sin_ref, qn_ref, wuq_ref, kvn_ref, wg_ref, bg_ref, wukt_ref,
                      ckv_ref, kpe_ref, la_ref, qlat_ref, qpe_ref, *, heads, scale, rope, **kw):
    q2, _, _, cos, sin = _prep_common(small_ref, cos_ref, sin_ref, qn_ref, wuq_ref, kvn_ref, wg_ref, bg_ref,
                                      ckv_ref, kpe_ref, la_ref, rope=rope, **kw)
    bt, _, lt, c_lat = qlat_ref.shape
    hw = heads * LANES
    for h in range(heads):
        c0 = h * LANES
        qh = (q2[:, c0:c0 + LANES] * cos + q2[:, hw + c0:hw + c0 + LANES] * sin) * scale
        qlat = _dot(qh.astype(BF16), wukt_ref[h])
        qlat_ref[:, h] = qlat.reshape(bt, lt, c_lat)
        qpe_ref[:, h] = qh[:, :rope].reshape(bt, lt, rope)


def _prep(small, cos_t, sin_t, b, l, wts, dims, sample):
    heads, q_lora, kv_lora, rope, nope = dims
    bt, lt = _token_tiles(b, l)
    nl = l // lt
    rows = bt * lt
    t = b * l
    scale = float(nope + rope) ** -0.5 * (1.0 if sample else LOG2_E)
    tab = pl.BlockSpec((lt, LANES), lambda bi, li: (li, 0))
    common_w = [wts["q_norm"], wts["w_uq2"], wts["kv_norm"], wts["w_gate"], wts["b_gate"]]
    common_out_specs = [_rows_spec(bt, lt, nl, kv_lora), _rows_spec(bt, lt, nl, rope),
                        _rows_spec(bt, lt, nl, wts["w_gate"].shape[1])]
    common_out_shape = [jax.ShapeDtypeStruct((t, kv_lora), F32), jax.ShapeDtypeStruct((t, rope), F32),
                        jax.ShapeDtypeStruct((t, wts["w_gate"].shape[1]), F32)]
    kw = dict(heads=heads, scale=scale, q_lora=q_lora, kv_lora=kv_lora, rope=rope)
    if not sample:
        extra_w = [wts["w_uk_p"], wts["w_uv"]]
        hw = heads * LANES
        nv = wts["w_uv"].shape[1]
        out_specs = common_out_specs + [_rows_spec(bt, lt, nl, hw), _rows_spec(bt, lt, nl, hw),
                                        _rows_spec(bt, lt, nl, nv)]
        out_shape = common_out_shape + [jax.ShapeDtypeStruct((t, hw), BF16), jax.ShapeDtypeStruct((t, hw), BF16),
                                        jax.ShapeDtypeStruct((t, nv), BF16)]
        body = functools.partial(_prep_prompt_body, **kw)
    else:
        extra_w = [wts["w_ukt_p"]]
        out_specs = common_out_specs + [
            pl.BlockSpec((bt, heads, lt, kv_lora), lambda bi, li: (bi, 0, li, 0)),
            pl.BlockSpec((bt, heads, lt, rope), lambda bi, li: (bi, 0, li, 0))]
        out_shape = common_out_shape + [jax.ShapeDtypeStruct((b, heads, l, kv_lora), F32),
                                        jax.ShapeDtypeStruct((b, heads, l, rope), F32)]
        body = functools.partial(_prep_sample_body, **kw)
    weights = common_w + extra_w
    return pl.pallas_call(
        body,
        grid=(b // bt, nl),
        in_specs=[_rows_spec(bt, lt, nl, small.shape[1]), tab, tab] + [_resident(w.shape) for w in weights],
        out_specs=out_specs,
        out_shape=out_shape,
        compiler_params=_cparams(("parallel", "parallel")),
        name="prep_sample" if sample else "prep_prompt",
    )(small, cos_t, sin_t, *weights)


def _flash_body(qi_ref, kj_ref, q_ref, k_ref, v_ref, o_ref, m_ref, acc_ref, *, heads):
    i, j = qi_ref[pl.program_id(1)], kj_ref[pl.program_id(1)]
    tq, tk = q_ref.shape[1], k_ref.shape[1]
    rep = tk // LANES
    half = LANES // 2

    @pl.when(j == 0)
    def _():
        m_ref[...] = jnp.full(m_ref.shape, -jnp.inf, F32)
        acc_ref[...] = jnp.zeros(acc_ref.shape, F32)

    def step(masked):
        if masked:
            keep = lax.broadcasted_iota(jnp.int32, (tq, tk), 0) >= lax.broadcasted_iota(jnp.int32, (tq, tk), 1)
        for h in range(heads):
            c0 = h * LANES
            s = _dot_nt(q_ref[0, :, c0:c0 + LANES], k_ref[0, :, c0:c0 + LANES])
            if masked:
                s = jnp.where(keep, s, -jnp.inf)
            m_prev = m_ref[h]
            m_new = jnp.maximum(m_prev, jnp.max(s, axis=-1, keepdims=True))
            p = jnp.exp2(s - jnp.tile(m_new, (1, rep)))
            alpha = jnp.exp2(m_prev - m_new)
            acc_ref[h] = alpha * acc_ref[h] + _dot(p.astype(BF16), v_ref[0, :, c0:c0 + LANES])
            m_ref[h] = m_new

    @pl.when(j < i)
    def _():
        step(False)

    @pl.when(j == i)
    def _():
        step(True)
        low = lax.broadcasted_iota(jnp.int32, (tq, LANES), 1) < half
        for hp in range(heads // 2):
            a, b = acc_ref[2 * hp], acc_ref[2 * hp + 1]
            oa = a / pltpu.roll(a, half, 1)
            ob = b / pltpu.roll(b, half, 1)
            o_ref[0, :, hp * LANES:(hp + 1) * LANES] = jnp.where(low, oa, pltpu.roll(ob, half, 1)).astype(BF16)


def _flash(q, k, v, b, l, heads):
    t = min(ATTN_TILE, l)
    n = l // t
    hw = heads * LANES
    nv = hw // 2
    q3, k3, v3 = q.reshape(b, l, hw), k.reshape(b, l, hw), v.reshape(b, l, hw)
    pairs = [(i, j) for i in range(n) for j in range(i + 1)]
    qi = jnp.asarray([p[0] for p in pairs], jnp.int32)
    kj = jnp.asarray([p[1] for p in pairs], jnp.int32)
    grid_spec = pltpu.PrefetchScalarGridSpec(
        num_scalar_prefetch=2,
        grid=(b, len(pairs)),
        in_specs=[
            pl.BlockSpec((1, t, hw), lambda bi, s, qi_, kj_: (bi, qi_[s], 0)),
            pl.BlockSpec((1, t, hw), lambda bi, s, qi_, kj_: (bi, kj_[s], 0)),
            pl.BlockSpec((1, t, hw), lambda bi, s, qi_, kj_: (bi, kj_[s], 0)),
        ],
        out_specs=pl.BlockSpec((1, t, nv), lambda bi, s, qi_, kj_: (bi, qi_[s], 0)),
        scratch_shapes=[pltpu.VMEM((heads, t, LANES), F32), pltpu.VMEM((heads, t, LANES), F32)],
    )
    out = pl.pallas_call(
        functools.partial(_flash_body, heads=heads),
        grid_spec=grid_spec,
        out_shape=jax.ShapeDtypeStruct((b, l, nv), BF16),
        compiler_params=_cparams(("parallel", "arbitrary")),
        name="flash_prompt",
    )(qi, kj, q3, k3, v3)
    return out.reshape(b * l, nv)


def _decode_body(pt_ref, qlat_ref, qpe_ref, ckvn_ref, kpen_ref, ckv_hbm, kpet_hbm, o_ref,
                 kbuf, pbuf, sbuf, kn_ref, pn_ref, sem, *, layer, n_pages, chunk, page):
    b, nb = pl.program_id(0), pl.num_programs(0)
    slot, nslot = b % 2, 1 - b % 2
    _, heads, lq, c_lat = qlat_ref.shape
    rope = qpe_ref.shape[3]
    rows = heads * lq
    n_chunks = n_pages // chunk
    width = chunk * page
    ql = qlat_ref[0].reshape(rows, c_lat).astype(BF16)
    qp = qpe_ref[0].reshape(rows, rope).astype(BF16)

    def page_copies(seq, pg, sl):
        phys = pt_ref[seq * n_pages + pg]
        r0 = pl.multiple_of(pg * page, page)
        ci = pg // chunk
        return (pltpu.make_async_copy(ckv_hbm.at[layer, phys], kbuf.at[sl, pl.ds(r0, page), :], sem.at[sl, 0, ci]),
                pltpu.make_async_copy(kpet_hbm.at[layer, phys], pbuf.at[sl, :, pl.ds(r0, page)], sem.at[sl, 1, ci]))

    @pl.when(b == 0)
    def _():
        def issue(pg, carry):
            for cp in page_copies(0, pg, 0):
                cp.start()
            return carry
        lax.fori_loop(0, n_pages, issue, 0)

    kn_ref[...] = jnp.zeros(kn_ref.shape, F32)
    pn_ref[...] = jnp.zeros(pn_ref.shape, F32)
    kn_ref[0:lq, :] = ckvn_ref[...]
    pn_ref[0:lq, :] = kpen_ref[...]
    kn = kn_ref[...].astype(BF16)
    s_new = _dot_nt(ql, kn) + _dot_nt(qp, pn_ref[...].astype(BF16))
    key = lax.broadcasted_iota(jnp.int32, (rows, LANES), 1)
    qpos = lax.broadcasted_iota(jnp.int32, (rows, LANES), 0) % lq
    s_new = jnp.where(key <= qpos, s_new, -jnp.inf)

    def keys(i):
        c0 = pl.multiple_of(i * width, width)
        return c0, kbuf[slot, pl.ds(c0, width), :].astype(BF16)

    nxt = jnp.minimum(b + 1, nb - 1)

    def scores(i, m):
        for p in range(chunk):
            for cp in page_copies(b, i * chunk + p, slot):
                cp.wait()
        c0, kb = keys(i)
        s = _dot_nt(ql, kb) + _dot(qp, pbuf[slot, :, pl.ds(c0, width)].astype(BF16))
        sbuf[:, pl.ds(c0, width)] = s
        for t in range(width // LANES):
            m = jnp.maximum(m, s[:, t * LANES:(t + 1) * LANES])
        for p in range(chunk):
            for cp in page_copies(nxt, i * chunk + p, nslot):
                cp.start()
        return m

    m = lax.fori_loop(0, n_chunks, scores, s_new)
    m = jnp.broadcast_to(jnp.max(m, axis=-1, keepdims=True), (rows, LANES))
    p_new = jnp.exp(s_new - m)
    m_wide = jnp.tile(m, (1, width // LANES))

    def values(i, carry):
        l, acc = carry
        c0, kb = keys(i)
        p = jnp.exp(sbuf[:, pl.ds(c0, width)] - m_wide)
        for t in range(width // LANES):
            l = l + p[:, t * LANES:(t + 1) * LANES]
        return l, acc + _dot(p.astype(BF16), kb)

    l, acc = lax.fori_loop(0, n_chunks, values, (p_new, _dot(p_new.astype(BF16), kn)))
    o = acc / jnp.sum(l, axis=-1, keepdims=True)
    o_ref[0] = o.reshape(heads, lq, c_lat)

    @pl.when(b == nb - 1)
    def _():
        def drain(pg, carry):
            for cp in page_copies(nxt, pg, nslot):
                cp.wait()
            return carry
        lax.fori_loop(0, n_pages, drain, 0)


def _decode(qlat, qpe, ckv_new, kpe_new, cache_ckv, cache_kpe_t, page_table, layer):
    b, heads, lq, c_lat = qlat.shape
    rope = qpe.shape[3]
    n_pages = page_table.shape[1]
    page = cache_ckv.shape[2]
    chunk = min(PAGES_PER_CHUNK, n_pages)
    assert n_pages % chunk == 0 and lq % SUBLANES == 0 and lq <= LANES and page % LANES == 0
    rows = heads * lq
    t_past = n_pages * page
    any_spec = pl.BlockSpec(memory_space=pl.ANY)
    grid_spec = pltpu.PrefetchScalarGridSpec(
        num_scalar_prefetch=1,
        grid=(b,),
        in_specs=[
            pl.BlockSpec((1, heads, lq, c_lat), lambda bi, pt: (bi, 0, 0, 0)),
            pl.BlockSpec((1, heads, lq, rope), lambda bi, pt: (bi, 0, 0, 0)),
            pl.BlockSpec((lq, c_lat), lambda bi, pt: (bi, 0)),
            pl.BlockSpec((lq, rope), lambda bi, pt: (bi, 0)),
            any_spec, any_spec,
        ],
        out_specs=pl.BlockSpec((1, heads, lq, c_lat), lambda bi, pt: (bi, 0, 0, 0)),
        scratch_shapes=[pltpu.VMEM((2, t_past, c_lat), F32), pltpu.VMEM((2, rope, t_past), F32),
                        pltpu.VMEM((rows, t_past), F32),
                        pltpu.VMEM((LANES, c_lat), F32), pltpu.VMEM((LANES, rope), F32),
                        pltpu.SemaphoreType.DMA((2, 2, n_pages // chunk))],
    )
    return pl.pallas_call(
        functools.partial(_decode_body, layer=layer, n_pages=n_pages, chunk=chunk, page=page),
        grid_spec=grid_spec,
        out_shape=jax.ShapeDtypeStruct((b, heads, lq, c_lat), F32),
        compiler_params=_cparams(("arbitrary",)),
        name="decode_sample",
    )(page_table.reshape(-1), qlat, qpe, ckv_new, kpe_new, cache_ckv, cache_kpe_t)


def _gla_masks(c, seg):
    row = lax.broadcasted_iota(jnp.int32, (c, c), 0)
    col = lax.broadcasted_iota(jnp.int32, (c, c), 1)
    tri = ((row >= col) & (row // seg == col // seg)).astype(BF16)
    levels = []
    m = seg // 2
    while m >= SUBLANES:
        levels.append((m, (row // (2 * m) == col // (2 * m)) & (row % (2 * m) >= m) & (col % (2 * m) < m)))
        m //= 2
    shape3 = (c // SUBLANES, SUBLANES, c)
    sub = lax.broadcasted_iota(jnp.int32, shape3, 0)
    trow = lax.broadcasted_iota(jnp.int32, shape3, 1)
    col3 = lax.broadcasted_iota(jnp.int32, shape3, 2)
    diag = [(col3 == sub * SUBLANES + s) & (trow >= s) for s in range(SUBLANES)]
    return tri, levels, diag


def _gla_chunk_intra(q, k, v, g, masks):
    tri, levels, diag = masks
    c, dk = q.shape
    g_hi = g.astype(BF16)
    g_lo = (g - g_hi.astype(F32)).astype(BF16)
    b = _dot(tri, g_hi) + _dot(tri, g_lo)

    a = jnp.zeros((c, c), F32)
    for m, valid in levels:
        b3 = b.reshape(c // (2 * m), 2 * m, dk)
        ref = jnp.broadcast_to(b3[:, m - 1:m, :], b3.shape).reshape(c, dk)
        f = jnp.exp2(-jnp.abs(b - ref))
        pm = _dot_nt((q * f).astype(BF16), (k * f).astype(BF16))
        a = jnp.where(valid, pm, a)

    nsub = c // SUBLANES
    q3, k3, b3 = (x.reshape(nsub, SUBLANES, dk) for x in (q, k, b))
    a3 = a.reshape(nsub, SUBLANES, c)
    for s in range(SUBLANES):
        bs = jnp.broadcast_to(b3[:, s:s + 1, :], b3.shape)
        ks = jnp.broadcast_to(k3[:, s:s + 1, :], k3.shape)
        w = q3 * ks * jnp.exp2(jnp.minimum(b3 - bs, 0.0))
        a3 = jnp.where(diag[s], jnp.sum(w, axis=-1, keepdims=True), a3)
    return b, _dot(a3.reshape(c, c).astype(BF16), v.astype(BF16))


def _gla_body(*refs, c, nb, tl, seg, use_s0, scale):
    if use_s0:
        q_ref, k_ref, v_ref, g_ref, gn_ref, s0_ref, o_ref, sf_ref, st_ref = refs
    else:
        q_ref, k_ref, v_ref, g_ref, gn_ref, o_ref, sf_ref, st_ref = refs
    li = pl.program_id(2)
    dk, dv = q_ref.shape[1], v_ref.shape[1]

    @pl.when(li == 0)
    def _():
        for ib in range(nb):
            st_ref[ib] = s0_ref[ib, 0].T if use_s0 else jnp.zeros((dv, dk), F32)

    gn = gn_ref[...]
    n_chunks = nb * tl // c
    nseg = c // seg
    rowseg = lax.broadcasted_iota(jnp.int32, (c, 1), 0) // seg
    masks = _gla_masks(c, seg)
    for ci in range(n_chunks):
        r0 = ci * c
        q = q_ref[r0:r0 + c, :].astype(F32) * scale
        k = k_ref[r0:r0 + c, :].astype(F32)
        v = v_ref[r0:r0 + c, :].astype(F32)
        g = g_ref[r0:r0 + c, :] * LOG2_E
        b, o = _gla_chunk_intra(q, k, v, g, masks)
        b3 = b.reshape(nseg, seg, dk)
        b_last = jnp.broadcast_to(b3[:, seg - 1:seg, :], b3.shape).reshape(c, dk)
        qe = (q * jnp.exp2(b)).astype(BF16)
        ke = (k * jnp.exp2(b_last - b)).astype(BF16)
        vt = v.T.astype(BF16)
        for sg in range(nseg):
            ib = (ci * nseg + sg) if nseg > 1 else (ci * c) // tl
            st = st_ref[ib]
            if nseg > 1:
                own = rowseg == sg
                o = o + jnp.where(own, _dot_nt(qe, st.astype(BF16)), 0.0)
                ke_s = jnp.where(own, ke, jnp.zeros_like(ke))
            else:
                o = o + _dot_nt(qe, st.astype(BF16))
                ke_s = ke
            dec = jnp.exp2(b[(sg + 1) * seg - 1:(sg + 1) * seg, :])
            st_ref[ib] = st * dec + _dot(vt, ke_s)
        o_ref[r0:r0 + c, :] = (_rms(o) * gn).astype(BF16)

    @pl.when(li == pl.num_programs(2) - 1)
    def _():
        for ib in range(nb):
            sf_ref[ib, 0] = st_ref[ib].T


def _gla(big, log_a, gla_norm, s0, b, l, heads, dk, dv, cols):
    c = GLA_CHUNK if l >= GLA_CHUNK else GLA_SHORT_CHUNK
    if l >= c:
        assert l % c == 0
        nb, seg = 1, c
        tl = min(ROW_TILE, l)
    else:
        assert c % l == 0 and l % SUBLANES == 0
        nb, seg, tl = c // l, l, l
        assert b % nb == 0
    nl = l // tl
    rows = nb * tl
    q0, k0, v0 = cols

    def tok(width, base):
        return pl.BlockSpec((rows, width), lambda bi, h, li: (bi * nl + li, base // width + h))

    in_specs = [tok(dk, q0), tok(dk, k0), tok(dv, v0), tok(dk, 0), pl.BlockSpec((1, dv), lambda bi, h, li: (0, 0))]
    args = [big, big, big, log_a, gla_norm]
    state_spec = pl.BlockSpec((nb, 1, dk, dv), lambda bi, h, li: (bi, h, 0, 0))
    if s0 is not None:
        in_specs.append(state_spec)
        args.append(s0)
    return pl.pallas_call(
        functools.partial(_gla_body, c=c, nb=nb, tl=tl, seg=seg, use_s0=s0 is not None, scale=float(dk) ** -0.5),
        grid=(b // nb, heads, nl),
        in_specs=in_specs,
        out_specs=[pl.BlockSpec((rows, dv), lambda bi, h, li: (bi * nl + li, h)), state_spec],
        out_shape=[jax.ShapeDtypeStruct((b * l, heads * dv), BF16), jax.ShapeDtypeStruct((b, heads, dk, dv), F32)],
        scratch_shapes=[pltpu.VMEM((nb, dv, dk), F32)],
        compiler_params=_cparams(("parallel", "parallel", "arbitrary")),
        name="gla",
    )(*args)


def _post_body(*refs, sample, alpha):
    if sample:
        (x_ref, gt_ref, olat_ref, ogla_ref, gr_ref, bra_ref, brb_ref, wuv_ref,
         wmla_ref, wgla_ref, wout_ref, g_ref, b_ref, o_ref) = refs
        bt, heads, lt, c_lat = olat_ref.shape
        o_mla = _dot(olat_ref[:, 0].reshape(bt * lt, c_lat).astype(BF16), wuv_ref[0])
        for h in range(1, heads):
            o_mla = o_mla + _dot(olat_ref[:, h].reshape(bt * lt, c_lat).astype(BF16), wuv_ref[h])
        o_mla = o_mla.astype(BF16)
    else:
        (x_ref, gt_ref, omla_ref, ogla_ref, gr_ref, bra_ref, brb_ref,
         wmla_ref, wgla_ref, wout_ref, g_ref, b_ref, o_ref) = refs
        o_mla = omla_ref[...]
    bt, lt, d = x_ref.shape
    u_mla = _dot(o_mla, wmla_ref[...])
    gr = gr_ref[...].astype(F32)
    og = (ogla_ref[...].astype(F32) * (gr * _sigmoid(gr))).astype(BF16)
    u_gla = _dot(og, wgla_ref[...])
    merged = _sigmoid(bra_ref[...].astype(F32)) * u_mla + _sigmoid(brb_ref[...].astype(F32)) * u_gla
    mix = _dot(merged.astype(BF16), wout_ref[...]).reshape(bt, lt, d)
    y = alpha * x_ref[...] + (1.0 + gt_ref[...]) * mix
    o_ref[...] = _ln(y) * g_ref[...] + b_ref[...]


def _post(x, mod, o_mla, o_gla, big, wts, cols, alpha, sample):
    b, l, d = x.shape
    bt, lt = _token_tiles(b, l)
    nl = l // lt
    gr0, br0 = cols
    if sample:
        heads, c_lat = o_mla.shape[1], o_mla.shape[3]
        mla_spec = pl.BlockSpec((bt, heads, lt, c_lat), lambda bi, li: (bi, 0, li, 0))
        weights = [wts["w_uv_p"], wts["w_mla_br"], wts["w_gla_br"], wts["w_out"], wts["ln1_g"], wts["ln1_b"]]
    else:
        mla_spec = _rows_spec(bt, lt, nl, o_mla.shape[1])
        weights = [wts["w_mla_br"], wts["w_gla_br"], wts["w_out"], wts["ln1_g"], wts["ln1_b"]]
    in_specs = [_x_spec(bt, lt, d), _mod_spec(bt, d, 2), mla_spec, _rows_spec(bt, lt, nl, o_gla.shape[1]),
                _rows_spec(bt, lt, nl, d, gr0 // d), _rows_spec(bt, lt, nl, d, br0 // d),
                _rows_spec(bt, lt, nl, d, br0 // d + 1)]
    return pl.pallas_call(
        functools.partial(_post_body, sample=sample, alpha=alpha),
        grid=(b // bt, nl),
        in_specs=in_specs + [_resident(w.shape) for w in weights],
        out_specs=_x_spec(bt, lt, d),
        out_shape=jax.ShapeDtypeStruct((b, l, d), F32),
        compiler_params=_cparams(("parallel", "parallel")),
        name="merge_sample" if sample else "merge_prompt",
    )(x, mod, o_mla, o_gla, big, big, big, *weights)


def _mlp_body(x_ref, sc_ref, sh_ref, gt_ref, wup_ref, wdn_ref, g_ref, b_ref, o_ref, *, alpha, chunk):
    bt, lt, d = x_ref.shape
    x = x_ref[...]
    hb = (_ln(x) * (1.0 + sc_ref[...]) + sh_ref[...]).reshape(bt * lt, d).astype(BF16)
    f = jnp.zeros((bt * lt, d), F32)
    for c0 in range(0, wup_ref.shape[1], chunk):
        u = jnp.maximum(_dot(hb, wup_ref[:, c0:c0 + chunk]), 0.0)
        f = f + _dot((u * u).astype(BF16), wdn_ref[c0:c0 + chunk, :])
    y = alpha * x + (1.0 + gt_ref[...]) * f.reshape(bt, lt, d)
    o_ref[...] = _ln(y) * g_ref[...] + b_ref[...]


def _mlp(x, mod, wts, alpha):
    b, l, d = x.shape
    bt, lt = _token_tiles(b, l)
    weights = [wts["w_up"], wts["w_down"], wts["ln2_g"], wts["ln2_b"]]
    return pl.pallas_call(
        functools.partial(_mlp_body, alpha=alpha, chunk=1024),
        grid=(b // bt, l // lt),
        in_specs=[_x_spec(bt, lt, d), _mod_spec(bt, d, 4), _mod_spec(bt, d, 3), _mod_spec(bt, d, 5)]
        + [_resident(w.shape) for w in weights],
        out_specs=_x_spec(bt, lt, d),
        out_shape=jax.ShapeDtypeStruct((b, l, d), F32),
        compiler_params=_cparams(("parallel", "parallel")),
        name="mlp",
    )(x, mod, mod, mod, *weights)


def _rot_cols(w):
    half = w.shape[-1] // 2
    return jnp.concatenate([-w[..., half:], w[..., :half]], axis=-1)


def _pad_cols(w, width):
    return jnp.pad(w, [(0, 0)] * (w.ndim - 1) + [(0, width - w.shape[-1])])


def _rope_tables(pos, rope):
    half = rope // 2
    inv = ROPE_THETA ** (-jnp.arange(half, dtype=F32) / half)
    ang = pos.astype(F32)[:, None] * inv[None, :]
    cos, sin = jnp.cos(ang), jnp.sin(ang)
    n = pos.shape[0]
    cos_t = jnp.concatenate([cos, cos, jnp.ones((n, LANES - 2 * rope), F32), jnp.zeros((n, rope), F32)], axis=1)
    sin_t = jnp.concatenate([sin, sin, jnp.zeros((n, LANES - rope), F32)], axis=1)
    return cos_t, sin_t


def kernel(x_prompt, x_sample, cache_ckv, cache_kpe, state_gla, page_table, c_prompt, c_sample, w_ada, b_ada, w_in, q_norm, kv_norm, w_uq, w_uk, w_uv, w_mla_br, w_gate_up, b_gate, gla_norm, w_gla_br, w_out, ln1_g, ln1_b, w_up, w_down, ln2_g, ln2_b):
    depth, d, _ = w_in.shape
    q_lora, kv_lora = q_norm.shape[1], kv_norm.shape[1]
    rope = cache_kpe.shape[3]
    heads, nope = w_uk.shape[2], w_uk.shape[3]
    v_dim = w_uv.shape[3]
    lowrank = w_gate_up.shape[1]
    dv = gla_norm.shape[1]
    gheads = w_gla_br.shape[1] // dv
    dk = w_gate_up.shape[2] // gheads
    alpha = (2.0 * depth) ** 0.25
    assert rope + nope <= LANES and 2 * v_dim == LANES and heads % 2 == 0 and lowrank <= LANES

    splits = (q_lora, kv_lora, rope, gheads * dk, gheads * dk, gheads * dv, lowrank, gheads * dv, 2 * d)
    offs = [0]
    for s in splits:
        offs.append(offs[-1] + s)
    seg = [w_in[:, :, offs[i]:offs[i + 1]] for i in range(len(splits))]
    w_qdn, w_kvdn, w_kr, w_gq, w_gk, w_gv, w_ga, w_gr, w_br = seg
    small_cols = [w_qdn, w_kvdn, _pad_cols(w_kr, LANES), _pad_cols(_rot_cols(w_kr), LANES), _pad_cols(w_ga, LANES)]
    n_small = sum(w.shape[2] for w in small_cols)
    assert n_small % 512 == 0 and q_lora % LANES == 0 and kv_lora % LANES == 0
    w_in_r = jnp.concatenate(small_cols + [w_gq, w_gk, w_gv, w_gr, w_br], axis=2).astype(BF16)
    q0, k0 = 0, gheads * dk
    v0 = 2 * gheads * dk
    gr0 = v0 + gheads * dv
    br0 = gr0 + gheads * dv

    w_uq_h = w_uq.reshape(depth, q_lora, heads, nope + rope)
    w_q_nope, w_q_rope = w_uq_h[..., :nope], w_uq_h[..., nope:]
    zpad = jnp.zeros((depth, q_lora, heads, LANES - rope - nope), F32)
    w_q_main = jnp.concatenate([w_q_rope, w_q_nope, zpad], axis=-1)
    w_q_rot = jnp.concatenate([_rot_cols(w_q_rope), jnp.zeros_like(w_q_nope), zpad], axis=-1)
    w_uq2 = jnp.concatenate([w_q_main.reshape(depth, q_lora, heads * LANES),
                             w_q_rot.reshape(depth, q_lora, heads * LANES)], axis=-1).astype(BF16)
    w_uk_p = jnp.pad(w_uk, ((0, 0), (0, 0), (0, 0), (rope, LANES - rope - nope)))
    w_ukt_p = jnp.transpose(w_uk_p, (0, 2, 3, 1)).astype(BF16)
    w_uk_p = w_uk_p.reshape(depth, kv_lora, heads * LANES).astype(BF16)
    w_uv_f = _pad_cols(w_uv, LANES).reshape(depth, kv_lora, heads * LANES).astype(BF16)
    eye = jnp.eye(heads, dtype=F32)
    w_uv_p = (w_uv.transpose(0, 2, 1, 3)[:, :, :, None, :] * eye[None, :, None, :, None]).reshape(
        depth, heads, kv_lora, heads * v_dim).astype(BF16)
    w_gate = jnp.pad(w_gate_up, ((0, 0), (0, LANES - lowrank), (0, 0))).astype(BF16)

    def layer_weights(l):
        return dict(
            q_norm=q_norm[l][None], kv_norm=kv_norm[l][None], w_uq2=w_uq2[l], w_uk_p=w_uk_p[l], w_ukt_p=w_ukt_p[l],
            w_uv=w_uv_f[l], w_uv_p=w_uv_p[l], w_gate=w_gate[l], b_gate=b_gate[l][None],
            w_mla_br=w_mla_br[l].astype(BF16), w_gla_br=w_gla_br[l].astype(BF16), w_out=w_out[l].astype(BF16),
            ln1_g=ln1_g[l][None], ln1_b=ln1_b[l][None], w_up=w_up[l].astype(BF16), w_down=w_down[l].astype(BF16),
            ln2_g=ln2_g[l][None], ln2_b=ln2_b[l][None])

    bp, lp, _ = x_prompt.shape
    bs, ls, _ = x_sample.shape
    past_len = page_table.shape[1] * cache_ckv.shape[2]
    mod_all = _ada(jnp.concatenate([c_prompt, c_sample], axis=0), w_ada, b_ada)
    tabs_p = _rope_tables(jnp.arange(lp, dtype=jnp.int32), rope)
    tabs_s = _rope_tables(past_len + jnp.arange(ls, dtype=jnp.int32), rope)
    dims = (heads, q_lora, kv_lora, rope, nope)
    cache_kpe_t = jnp.swapaxes(cache_kpe, 2, 3)

    xp, xs = x_prompt, x_sample
    outs = {k: [] for k in ("ckv_p", "kpe_p", "gla_p", "ckv_s", "kpe_s", "gla_s")}
    for l in range(depth):
        wts = layer_weights(l)
        gn = gla_norm[l][None]
        mod = mod_all[l, :bp][:, None, :]
        small, big = _inproj(xp, mod, w_in_r[l], n_small)
        ckv, kpe, log_a, q, k, v = _prep(small, *tabs_p, bp, lp, wts, dims, sample=False)
        o_mla = _flash(q, k, v, bp, lp, heads)
        o_gla, s_new = _gla(big, log_a, gn, None, bp, lp, gheads, dk, dv, (q0, k0, v0))
        x1 = _post(xp, mod, o_mla, o_gla, big, wts, (gr0, br0), alpha, sample=False)
        xp = _mlp(x1, mod, wts, alpha)
        outs["ckv_p"].append(ckv.reshape(bp, lp, kv_lora))
        outs["kpe_p"].append(kpe.reshape(bp, lp, rope))
        outs["gla_p"].append(s_new)
        mod = mod_all[l, bp:][:, None, :]
        small, big = _inproj(xs, mod, w_in_r[l], n_small)
        ckv, kpe, log_a, qlat, qpe = _prep(small, *tabs_s, bs, ls, wts, dims, sample=True)
        o_lat = _decode(qlat, qpe, ckv, kpe, cache_ckv, cache_kpe_t, page_table, l)
        o_gla, s_new = _gla(big, log_a, gn, state_gla[l], bs, ls, gheads, dk, dv, (q0, k0, v0))
        x1 = _post(xs, mod, o_lat, o_gla, big, wts, (gr0, br0), alpha, sample=True)
        xs = _mlp(x1, mod, wts, alpha)
        outs["ckv_s"].append(ckv.reshape(bs, ls, kv_lora))
        outs["kpe_s"].append(kpe.reshape(bs, ls, rope))
        outs["gla_s"].append(s_new)

    st = {k: jnp.stack(v) for k, v in outs.items()}
    return (xp, xs, st["ckv_p"], st["kpe_p"], st["gla_p"], st["ckv_s"], st["kpe_s"], st["gla_s"])
```

```python
import functools

import jax
import jax.numpy as jnp
from jax import lax
from jax.experimental import pallas as pl
from jax.experimental.pallas import tpu as pltpu

F32 = jnp.float32
BF16 = jnp.bfloat16

LANES = 128
SUBLANES = 8
ROPE_THETA = 10000.0
GLA_TAU = 16.0
LOG2_E = 1.4426950408889634
GLA_CHUNK = 128
GLA_SHORT_CHUNK = 64
EPS = 1e-6
N_ADA = 6
ROW_TILE = 512
ATTN_TILE = 512
PAGES_PER_CHUNK = 32
VMEM_LIMIT = 56 * 1024 * 1024


def _cparams(sem):
    return pltpu.CompilerParams(dimension_semantics=sem, vmem_limit_bytes=VMEM_LIMIT)


def _resident(shape):
    nd = len(shape)
    return pl.BlockSpec(shape, lambda *_: (0,) * nd, pipeline_mode=pl.Buffered(1))


def _dot(a, b):
    return jnp.dot(a, b, preferred_element_type=F32)


def _dot_nt(a, b):
    return lax.dot_general(a, b, (((1,), (1,)), ((), ())), preferred_element_type=F32)


def _ln(x):
    mu = jnp.mean(x, axis=-1, keepdims=True)
    xc = x - mu
    var = jnp.mean(xc * xc, axis=-1, keepdims=True)
    return xc * lax.rsqrt(var + EPS)


def _rms(x):
    return x * lax.rsqrt(jnp.mean(x * x, axis=-1, keepdims=True) + EPS)


def _sigmoid(x):
    return 1.0 / (1.0 + jnp.exp(-x))


def _log_sigmoid(x):
    return jnp.minimum(x, 0.0) - jnp.log(1.0 + jnp.exp(-jnp.abs(x)))


def _ada_body(c_ref, w_ref, b_ref, o_ref):
    c = c_ref[...]
    s = (c * _sigmoid(c)).astype(BF16)
    o_ref[0] = _dot(s, w_ref[0].astype(BF16)) + b_ref[0]


def _ada(c_all, w_ada, b_ada):
    depth, d, n = w_ada.shape
    rows = c_all.shape[0]
    tn = 1536
    return pl.pallas_call(
        _ada_body,
        grid=(depth, n // tn),
        in_specs=[
            pl.BlockSpec((rows, d), lambda l, j: (0, 0)),
            pl.BlockSpec((1, d, tn), lambda l, j: (l, 0, j)),
            pl.BlockSpec((1, 1, tn), lambda l, j: (l, 0, j)),
        ],
        out_specs=pl.BlockSpec((1, rows, tn), lambda l, j: (l, 0, j)),
        out_shape=jax.ShapeDtypeStruct((depth, rows, n), F32),
        compiler_params=_cparams(("arbitrary", "arbitrary")),
        name="ada_mod",
    )(c_all, w_ada, b_ada.reshape(depth, 1, n))


def _token_tiles(b, l):
    if l >= ROW_TILE:
        assert l % ROW_TILE == 0
        return 1, ROW_TILE
    bt = min(b, ROW_TILE // l)
    assert b % bt == 0
    return bt, l


def _x_spec(bt, lt, d):
    return pl.BlockSpec((bt, lt, d), lambda b, l: (b, l, 0))


def _mod_spec(bt, d, k):
    return pl.BlockSpec((bt, 1, d), lambda b, l: (b, 0, k))


def _rows_spec(bt, lt, nl, n, col=0):
    return pl.BlockSpec((bt * lt, n), lambda b, l: (b * nl + l, col))


def _inproj_body(x_ref, sc_ref, sh_ref, w_ref, small_ref, big_ref, *, n_small, chunk):
    bt, lt, d = x_ref.shape
    h = _ln(x_ref[...]) * (1.0 + sc_ref[...]) + sh_ref[...]
    hb = h.reshape(bt * lt, d).astype(BF16)
    for c0 in range(0, w_ref.shape[1], chunk):
        acc = _dot(hb, w_ref[:, c0:c0 + chunk])
        if c0 < n_small:
            small_ref[:, c0:c0 + chunk] = acc
        else:
            big_ref[:, c0 - n_small:c0 - n_small + chunk] = acc.astype(BF16)


def _inproj(x, mod, w_in_r, n_small):
    b, l, d = x.shape
    bt, lt = _token_tiles(b, l)
    nl = l // lt
    n_big = w_in_r.shape[1] - n_small
    return pl.pallas_call(
        functools.partial(_inproj_body, n_small=n_small, chunk=512),
        grid=(b // bt, nl),
        in_specs=[_x_spec(bt, lt, d), _mod_spec(bt, d, 1), _mod_spec(bt, d, 0), _resident(w_in_r.shape)],
        out_specs=[_rows_spec(bt, lt, nl, n_small), _rows_spec(bt, lt, nl, n_big)],
        out_shape=[jax.ShapeDtypeStruct((b * l, n_small), F32), jax.ShapeDtypeStruct((b * l, n_big), BF16)],
        compiler_params=_cparams(("parallel", "parallel")),
        name="in_proj",
    )(x, mod, mod, w_in_r)


def _prep_common(small_ref, cos_ref, sin_ref, qn_ref, wuq_ref, kvn_ref, wg_ref, bg_ref,
                 ckv_ref, kpe_ref, la_ref, *, q_lora, kv_lora, rope):
    rows = small_ref.shape[0]
    lt = cos_ref.shape[0]
    s = small_ref[...]
    o = q_lora + kv_lora
    q_dn, kv_dn = s[:, :q_lora], s[:, q_lora:o]
    kr, krr, ga = s[:, o:o + LANES], s[:, o + LANES:o + 2 * LANES], s[:, o + 2 * LANES:o + 3 * LANES]
    cos, sin = cos_ref[...], sin_ref[...]
    if rows != lt:
        cos = jnp.broadcast_to(cos[None], (rows // lt, lt, LANES)).reshape(rows, LANES)
        sin = jnp.broadcast_to(sin[None], (rows // lt, lt, LANES)).reshape(rows, LANES)
    qn = (_rms(q_dn) * qn_ref[...]).astype(BF16)
    q2 = _dot(qn, wuq_ref[...])
    ckv = _rms(kv_dn) * kvn_ref[...]
    ckv_ref[...] = ckv
    kpe = kr * cos + krr * sin
    kpe_ref[...] = kpe[:, :rope]
    xg = _dot(ga.astype(BF16), wg_ref[...]) + bg_ref[...]
    la_ref[...] = _log_sigmoid(xg) * (1.0 / GLA_TAU)
    return q2, ckv, kpe, cos, sin


def _prep_prompt_body(small_ref, cos_ref, sin_ref, qn_ref, wuq_ref, kvn_ref, wg_ref, bg_ref, wuk_ref, wuv_ref,
                      ckv_ref, kpe_ref, la_ref, q_ref, k_ref, v_ref, *, heads, scale, **kw):
    q2, ckv, kpe, cos, sin = _prep_common(small_ref, cos_ref, sin_ref, qn_ref, wuq_ref, kvn_ref, wg_ref, bg_ref,
                                          ckv_ref, kpe_ref, la_ref, **kw)
    cb = ckv.astype(BF16)
    kn = _dot(cb, wuk_ref[...])
    hw = heads * LANES
    v = _dot(cb, wuv_ref[...])
    upper = lax.broadcasted_iota(jnp.int32, v.shape, 1) % LANES >= LANES // 2
    v_ref[...] = jnp.where(upper, 1.0, v).astype(BF16)
    for h in range(heads):
        c0 = h * LANES
        qh = (q2[:, c0:c0 + LANES] * cos + q2[:, hw + c0:hw + c0 + LANES] * sin) * scale
        q_ref[:, c0:c0 + LANES] = qh.astype(BF16)
        k_ref[:, c0:c0 + LANES] = (kn[:, c0:c0 + LANES] + kpe).astype(BF16)


def _prep_sample_body(small_ref, cos_ref, sin_ref, qn_ref, wuq_ref, kvn_ref, wg_ref, bg_ref, wukt_ref,
                      ckv_ref, kpe_ref, la_ref, qlat_ref, qpe_ref, *, heads, scale, rope, **kw):
    q2, _, _, cos, sin = _prep_common(small_ref, cos_ref, sin_ref, qn_ref, wuq_ref, kvn_ref, wg_ref, bg_ref,
                                      ckv_ref, kpe_ref, la_ref, rope=rope, **kw)
    bt, _, lt, c_lat = qlat_ref.shape
    hw = heads * LANES
    for h in range(heads):
        c0 = h * LANES
        qh = (q2[:, c0:c0 + LANES] * cos + q2[:, hw + c0:hw + c0 + LANES] * sin) * scale
        qlat = _dot(qh.astype(BF16), wukt_ref[h])
        qlat_ref[:, h] = qlat.reshape(bt, lt, c_lat)
        qpe_ref[:, h] = qh[:, :rope].reshape(bt, lt, rope)


def _prep(small, cos_t, sin_t, b, l, wts, dims, sample):
    heads, q_lora, kv_lora, rope, nope = dims
    bt, lt = _token_tiles(b, l)
    nl = l // lt
    rows = bt * lt
    t = b * l
    scale = float(nope + rope) ** -0.5 * (1.0 if sample else LOG2_E)
    tab = pl.BlockSpec((lt, LANES), lambda bi, li: (li, 0))
    common_w = [wts["q_norm"], wts["w_uq2"], wts["kv_norm"], wts["w_gate"], wts["b_gate"]]
    common_out_specs = [_rows_spec(bt, lt, nl, kv_lora), _rows_spec(bt, lt, nl, rope),
                        _rows_spec(bt, lt, nl, wts["w_gate"].shape[1])]
    common_out_shape = [jax.ShapeDtypeStruct((t, kv_lora), F32), jax.ShapeDtypeStruct((t, rope), F32),
                        jax.ShapeDtypeStruct((t, wts["w_gate"].shape[1]), F32)]
    kw = dict(heads=heads, scale=scale, q_lora=q_lora, kv_lora=kv_lora, rope=rope)
    if not sample:
        extra_w = [wts["w_uk_p"], wts["w_uv"]]
        hw = heads * LANES
        nv = wts["w_uv"].shape[1]
        out_specs = common_out_specs + [_rows_spec(bt, lt, nl, hw), _rows_spec(bt, lt, nl, hw),
                                        _rows_spec(bt, lt, nl, nv)]
        out_shape = common_out_shape + [jax.ShapeDtypeStruct((t, hw), BF16), jax.ShapeDtypeStruct((t, hw), BF16),
                                        jax.ShapeDtypeStruct((t, nv), BF16)]
        body = functools.partial(_prep_prompt_body, **kw)
    else:
        extra_w = [wts["w_ukt_p"]]
        out_specs = common_out_specs + [
            pl.BlockSpec((bt, heads, lt, kv_lora), lambda bi, li: (bi, 0, li, 0)),
            pl.BlockSpec((bt, heads, lt, rope), lambda bi, li: (bi, 0, li, 0))]
        out_shape = common_out_shape + [jax.ShapeDtypeStruct((b, heads, l, kv_lora), F32),
                                        jax.ShapeDtypeStruct((b, heads, l, rope), F32)]
        body = functools.partial(_prep_sample_body, **kw)
    weights = common_w + extra_w
    return pl.pallas_call(
        body,
        grid=(b // bt, nl),
        in_specs=[_rows_spec(bt, lt, nl, small.shape[1]), tab, tab] + [_resident(w.shape) for w in weights],
        out_specs=out_specs,
        out_shape=out_shape,
        compiler_params=_cparams(("parallel", "parallel")),
        name="prep_sample" if sample else "prep_prompt",
    )(small, cos_t, sin_t, *weights)


def _flash_body(qi_ref, kj_ref, q_ref, k_ref, v_ref, o_ref, m_ref, acc_ref, *, heads):
    i, j = qi_ref[pl.program_id(1)], kj_ref[pl.program_id(1)]
    tq, tk = q_ref.shape[1], k_ref.shape[1]
    half = LANES // 2

    @pl.when(j == 0)
    def _():
        m_ref[...] = jnp.full(m_ref.shape, -jnp.inf, F32)
        acc_ref[...] = jnp.zeros(acc_ref.shape, F32)

    def update(h, r0, nr, nc, keep):
        c0 = h * LANES
        s = _dot_nt(q_ref[0, r0:r0 + nr, c0:c0 + LANES], k_ref[0, 0:nc, c0:c0 + LANES])
        if keep is not None:
            s = jnp.where(keep, s, -jnp.inf)
        m_prev = m_ref[h, r0:r0 + nr]
        m_new = jnp.maximum(m_prev, jnp.max(s, axis=-1, keepdims=True))
        p = jnp.exp2(s - jnp.tile(m_new, (1, nc // LANES)))
        alpha = jnp.exp2(m_prev - m_new)
        acc_ref[h, r0:r0 + nr] = alpha * acc_ref[h, r0:r0 + nr] + _dot(p.astype(BF16), v_ref[0, 0:nc, c0:c0 + LANES])
        m_ref[h, r0:r0 + nr] = m_new

    def step(masked):
        keep = None
        if masked:
            keep = lax.broadcasted_iota(jnp.int32, (tq, tk), 0) >= lax.broadcasted_iota(jnp.int32, (tq, tk), 1)
        for h in range(heads):
            update(h, 0, tq, tk, keep)

    @pl.when(j < i)
    def _():
        step(False)

    @pl.when(j == i)
    def _():
        step(True)
        low = lax.broadcasted_iota(jnp.int32, (tq, LANES), 1) < half
        for hp in range(heads // 2):
            a, b = acc_ref[2 * hp], acc_ref[2 * hp + 1]
            oa = a / pltpu.roll(a, half, 1)
            ob = b / pltpu.roll(b, half, 1)
            o_ref[0, :, hp * LANES:(hp + 1) * LANES] = jnp.where(low, oa, pltpu.roll(ob, half, 1)).astype(BF16)


def _flash(q, k, v, b, l, heads):
    t = min(ATTN_TILE, l)
    n = l // t
    hw = heads * LANES
    nv = hw // 2
    q3, k3, v3 = q.reshape(b, l, hw), k.reshape(b, l, hw), v.reshape(b, l, hw)
    pairs = [(i, j) for i in range(n) for j in range(i + 1)]
    qi = jnp.asarray([p[0] for p in pairs], jnp.int32)
    kj = jnp.asarray([p[1] for p in pairs], jnp.int32)
    grid_spec = pltpu.PrefetchScalarGridSpec(
        num_scalar_prefetch=2,
        grid=(b, len(pairs)),
        in_specs=[
            pl.BlockSpec((1, t, hw), lambda bi, s, qi_, kj_: (bi, qi_[s], 0)),
            pl.BlockSpec((1, t, hw), lambda bi, s, qi_, kj_: (bi, kj_[s], 0)),
            pl.BlockSpec((1, t, hw), lambda bi, s, qi_, kj_: (bi, kj_[s], 0)),
        ],
        out_specs=pl.BlockSpec((1, t, nv), lambda bi, s, qi_, kj_: (bi, qi_[s], 0)),
        scratch_shapes=[pltpu.VMEM((heads, t, LANES), F32), pltpu.VMEM((heads, t, LANES), F32)],
    )
    out = pl.pallas_call(
        functools.partial(_flash_body, heads=heads),
        grid_spec=grid_spec,
        out_shape=jax.ShapeDtypeStruct((b, l, nv), BF16),
        compiler_params=_cparams(("parallel", "arbitrary")),
        name="flash_prompt",
    )(qi, kj, q3, k3, v3)
    return out.reshape(b * l, nv)


def _decode_body(pt_ref, qlat_ref, qpe_ref, ckvn_ref, kpen_ref, ckv_hbm, kpet_hbm, o_ref,
                 kbuf, pbuf, kb16, sbuf, kn_ref, pn_ref, sem, *, layer, n_pages, chunk, page):
    b, nb = pl.program_id(0), pl.num_programs(0)
    slot, nslot = b % 2, 1 - b % 2
    _, heads, lq, c_lat = qlat_ref.shape
    rope = qpe_ref.shape[3]
    rows = heads * lq
    n_chunks = n_pages // chunk
    width = chunk * page
    ql = qlat_ref[0].reshape(rows, c_lat).astype(BF16)
    qp = qpe_ref[0].reshape(rows, rope).astype(BF16)

    def page_copies(phys, ci, p, sl):
        r0 = pl.multiple_of((ci * chunk + p) * page, page)
        return (pltpu.make_async_copy(ckv_hbm.at[layer, phys], kbuf.at[sl, pl.ds(r0, page), :], sem.at[sl, 0, ci]),
                pltpu.make_async_copy(kpet_hbm.at[layer, phys], pbuf.at[sl, :, pl.ds(r0, page)], sem.at[sl, 1, ci]))

    @pl.when(b == 0)
    def _():
        def issue(ci, carry):
            for p in range(chunk):
                for cp in page_copies(pt_ref[ci * chunk + p], ci, p, 0):
                    cp.start()
            return carry
        lax.fori_loop(0, n_chunks, issue, 0)

    kn_ref[...] = jnp.zeros(kn_ref.shape, F32)
    pn_ref[...] = jnp.zeros(pn_ref.shape, F32)
    kn_ref[0:lq, :] = ckvn_ref[...]
    pn_ref[0:lq, :] = kpen_ref[...]
    kn = kn_ref[...].astype(BF16)
    s_new = _dot_nt(ql, kn) + _dot_nt(qp, pn_ref[...].astype(BF16))
    key = lax.broadcasted_iota(jnp.int32, (rows, LANES), 1)
    qpos = lax.broadcasted_iota(jnp.int32, (rows, LANES), 0) % lq
    s_new = jnp.where(key <= qpos, s_new, -jnp.inf)

    nxt = jnp.minimum(b + 1, nb - 1)

    def scores(i, m):
        for p in range(chunk):
            for cp in page_copies(0, i, p, slot):
                cp.wait()
        phys = [pt_ref[nxt * n_pages + i * chunk + p] for p in range(chunk)]
        for p in range(chunk):
            for cp in page_copies(phys[p], i, p, nslot):
                cp.start()
        c0 = pl.multiple_of(i * width, width)
        kb = kbuf[slot, pl.ds(c0, width), :].astype(BF16)
        kb16[pl.ds(c0, width), :] = kb
        s = _dot_nt(ql, kb) + _dot(qp, pbuf[slot, :, pl.ds(c0, width)].astype(BF16))
        sbuf[:, pl.ds(c0, width)] = s
        for t in range(width // LANES):
            m = jnp.maximum(m, s[:, t * LANES:(t + 1) * LANES])
        return m

    m = lax.fori_loop(0, n_chunks, scores, s_new)
    m = jnp.broadcast_to(jnp.max(m, axis=-1, keepdims=True), (rows, LANES))
    p_new = jnp.exp(s_new - m)
    m_wide = jnp.tile(m, (1, width // LANES))

    def values(i, carry):
        l, acc = carry
        c0 = pl.multiple_of(i * width, width)
        p = jnp.exp(sbuf[:, pl.ds(c0, width)] - m_wide)
        for t in range(width // LANES):
            l = l + p[:, t * LANES:(t + 1) * LANES]
        return l, acc + _dot(p.astype(BF16), kb16[pl.ds(c0, width), :])

    l, acc = lax.fori_loop(0, n_chunks, values, (p_new, _dot(p_new.astype(BF16), kn)))
    o = acc / jnp.sum(l, axis=-1, keepdims=True)
    o_ref[0] = o.reshape(heads, lq, c_lat)

    @pl.when(b == nb - 1)
    def _():
        def drain(ci, carry):
            for p in range(chunk):
                for cp in page_copies(0, ci, p, nslot):
                    cp.wait()
            return carry
        lax.fori_loop(0, n_chunks, drain, 0)


def _decode(qlat, qpe, ckv_new, kpe_new, cache_ckv, cache_kpe_t, page_table, layer):
    b, heads, lq, c_lat = qlat.shape
    rope = qpe.shape[3]
    n_pages = page_table.shape[1]
    page = cache_ckv.shape[2]
    chunk = min(PAGES_PER_CHUNK, n_pages)
    assert n_pages % chunk == 0 and lq % SUBLANES == 0 and lq <= LANES and page % LANES == 0
    rows = heads * lq
    t_past = n_pages * page
    any_spec = pl.BlockSpec(memory_space=pl.ANY)
    grid_spec = pltpu.PrefetchScalarGridSpec(
        num_scalar_prefetch=1,
        grid=(b,),
        in_specs=[
            pl.BlockSpec((1, heads, lq, c_lat), lambda bi, pt: (bi, 0, 0, 0)),
            pl.BlockSpec((1, heads, lq, rope), lambda bi, pt: (bi, 0, 0, 0)),
            pl.BlockSpec((lq, c_lat), lambda bi, pt: (bi, 0)),
            pl.BlockSpec((lq, rope), lambda bi, pt: (bi, 0)),
            any_spec, any_spec,
        ],
        out_specs=pl.BlockSpec((1, heads, lq, c_lat), lambda bi, pt: (bi, 0, 0, 0)),
        scratch_shapes=[pltpu.VMEM((2, t_past, c_lat), F32), pltpu.VMEM((2, rope, t_past), F32),
                        pltpu.VMEM((t_past, c_lat), BF16), pltpu.VMEM((rows, t_past), F32),
                        pltpu.VMEM((LANES, c_lat), F32), pltpu.VMEM((LANES, rope), F32),
                        pltpu.SemaphoreType.DMA((2, 2, n_pages // chunk))],
    )
    return pl.pallas_call(
        functools.partial(_decode_body, layer=layer, n_pages=n_pages, chunk=chunk, page=page),
        grid_spec=grid_spec,
        out_shape=jax.ShapeDtypeStruct((b, heads, lq, c_lat), F32),
        compiler_params=_cparams(("arbitrary",)),
        name="decode_sample",
    )(page_table.reshape(-1), qlat, qpe, ckv_new, kpe_new, cache_ckv, cache_kpe_t)


def _gla_masks(c, seg):
    row = lax.broadcasted_iota(jnp.int32, (c, c), 0)
    col = lax.broadcasted_iota(jnp.int32, (c, c), 1)
    tri = ((row >= col) & (row // seg == col // seg)).astype(BF16)
    levels = []
    m = seg // 2
    while m >= SUBLANES:
        levels.append((m, (row // (2 * m) == col // (2 * m)) & (row % (2 * m) >= m) & (col % (2 * m) < m)))
        m //= 2
    shape3 = (c // SUBLANES, SUBLANES, c)
    sub = lax.broadcasted_iota(jnp.int32, shape3, 0)
    trow = lax.broadcasted_iota(jnp.int32, shape3, 1)
    col3 = lax.broadcasted_iota(jnp.int32, shape3, 2)
    diag = [(col3 == sub * SUBLANES + s) & (trow >= s) for s in range(SUBLANES)]
    return tri, levels, diag


def _gla_chunk_intra(q, k, v, g, masks):
    tri, levels, diag = masks
    c, dk = q.shape
    g_hi = g.astype(BF16)
    g_lo = (g - g_hi.astype(F32)).astype(BF16)
    b = _dot(tri, g_hi) + _dot(tri, g_lo)

    a = jnp.zeros((c, c), F32)
    for m, valid in levels:
        b3 = b.reshape(c // (2 * m), 2 * m, dk)
        ref = jnp.broadcast_to(b3[:, m - 1:m, :], b3.shape).reshape(c, dk)
        f = jnp.exp2(-jnp.abs(b - ref))
        pm = _dot_nt((q * f).astype(BF16), (k * f).astype(BF16))
        a = jnp.where(valid, pm, a)

    nsub = c // SUBLANES
    q3, k3, b3 = (x.reshape(nsub, SUBLANES, dk) for x in (q, k, b))
    a3 = a.reshape(nsub, SUBLANES, c)
    for s in range(SUBLANES):
        bs = jnp.broadcast_to(b3[:, s:s + 1, :], b3.shape)
        ks = jnp.broadcast_to(k3[:, s:s + 1, :], k3.shape)
        w = q3 * ks * jnp.exp2(jnp.minimum(b3 - bs, 0.0))
        a3 = jnp.where(diag[s], jnp.sum(w, axis=-1, keepdims=True), a3)
    return b, _dot(a3.reshape(c, c).astype(BF16), v.astype(BF16))


def _gla_body(*refs, c, nb, tl, seg, use_s0, scale):
    if use_s0:
        q_ref, k_ref, v_ref, g_ref, gn_ref, s0_ref, o_ref, sf_ref, st_ref = refs
    else:
        q_ref, k_ref, v_ref, g_ref, gn_ref, o_ref, sf_ref, st_ref = refs
    li = pl.program_id(2)
    dk, dv = q_ref.shape[1], v_ref.shape[1]

    @pl.when(li == 0)
    def _():
        for ib in range(nb):
            st_ref[ib] = s0_ref[ib, 0].T if use_s0 else jnp.zeros((dv, dk), F32)

    gn = gn_ref[...]
    n_chunks = nb * tl // c
    nseg = c // seg
    rowseg = lax.broadcasted_iota(jnp.int32, (c, 1), 0) // seg
    masks = _gla_masks(c, seg)
    for ci in range(n_chunks):
        r0 = ci * c
        q = q_ref[r0:r0 + c, :].astype(F32) * scale
        k = k_ref[r0:r0 + c, :].astype(F32)
        v = v_ref[r0:r0 + c, :].astype(F32)
        g = g_ref[r0:r0 + c, :] * LOG2_E
        b, o = _gla_chunk_intra(q, k, v, g, masks)
        b3 = b.reshape(nseg, seg, dk)
        b_last = jnp.broadcast_to(b3[:, seg - 1:seg, :], b3.shape).reshape(c, dk)
        qe = (q * jnp.exp2(b)).astype(BF16)
        ke = (k * jnp.exp2(b_last - b)).astype(BF16)
        vt = v.T.astype(BF16)
        for sg in range(nseg):
            ib = (ci * nseg + sg) if nseg > 1 else (ci * c) // tl
            st = st_ref[ib]
            if nseg > 1:
                own = rowseg == sg
                o = o + jnp.where(own, _dot_nt(qe, st.astype(BF16)), 0.0)
                ke_s = jnp.where(own, ke, jnp.zeros_like(ke))
            else:
                o = o + _dot_nt(qe, st.astype(BF16))
                ke_s = ke
            dec = jnp.exp2(b[(sg + 1) * seg - 1:(sg + 1) * seg, :])
            st_ref[ib] = st * dec + _dot(vt, ke_s)
        o_ref[r0:r0 + c, :] = (_rms(o) * gn).astype(BF16)

    @pl.when(li == pl.num_programs(2) - 1)
    def _():
        for ib in range(nb):
            sf_ref[ib, 0] = st_ref[ib].T


def _gla(big, log_a, gla_norm, s0, b, l, heads, dk, dv, cols):
    c = GLA_CHUNK if l >= GLA_CHUNK else GLA_SHORT_CHUNK
    if l >= c:
        assert l % c == 0
        nb, seg = 1, c
        tl = min(ROW_TILE, l)
    else:
        assert c % l == 0 and l % SUBLANES == 0
        nb, seg, tl = c // l, l, l
        assert b % nb == 0
    nl = l // tl
    rows = nb * tl
    q0, k0, v0 = cols

    def tok(width, base):
        return pl.BlockSpec((rows, width), lambda bi, h, li: (bi * nl + li, base // width + h))

    in_specs = [tok(dk, q0), tok(dk, k0), tok(dv, v0), tok(dk, 0), pl.BlockSpec((1, dv), lambda bi, h, li: (0, 0))]
    args = [big, big, big, log_a, gla_norm]
    state_spec = pl.BlockSpec((nb, 1, dk, dv), lambda bi, h, li: (bi, h, 0, 0))
    if s0 is not None:
        s0_all, layer = s0
        in_specs.append(pl.BlockSpec((None, nb, 1, dk, dv), lambda bi, h, li: (layer, bi, h, 0, 0)))
        args.append(s0_all)
    return pl.pallas_call(
        functools.partial(_gla_body, c=c, nb=nb, tl=tl, seg=seg, use_s0=s0 is not None, scale=float(dk) ** -0.5),
        grid=(b // nb, heads, nl),
        in_specs=in_specs,
        out_specs=[pl.BlockSpec((rows, dv), lambda bi, h, li: (bi * nl + li, h)), state_spec],
        out_shape=[jax.ShapeDtypeStruct((b * l, heads * dv), BF16), jax.ShapeDtypeStruct((b, heads, dk, dv), F32)],
        scratch_shapes=[pltpu.VMEM((nb, dv, dk), F32)],
        compiler_params=_cparams(("parallel", "parallel", "arbitrary")),
        name="gla",
    )(*args)


def _post_body(*refs, sample, alpha):
    if sample:
        (x_ref, gt_ref, olat_ref, ogla_ref, gr_ref, bra_ref, brb_ref, wuv_ref,
         wmla_ref, wgla_ref, wout_ref, g_ref, b_ref, o_ref) = refs
        bt, heads, lt, c_lat = olat_ref.shape
        o_mla = _dot(olat_ref[:, 0].reshape(bt * lt, c_lat).astype(BF16), wuv_ref[0])
        for h in range(1, heads):
            o_mla = o_mla + _dot(olat_ref[:, h].reshape(bt * lt, c_lat).astype(BF16), wuv_ref[h])
        o_mla = o_mla.astype(BF16)
    else:
        (x_ref, gt_ref, omla_ref, ogla_ref, gr_ref, bra_ref, brb_ref,
         wmla_ref, wgla_ref, wout_ref, g_ref, b_ref, o_ref) = refs
        o_mla = omla_ref[...]
    bt, lt, d = x_ref.shape
    u_mla = _dot(o_mla, wmla_ref[...])
    gr = gr_ref[...].astype(F32)
    og = (ogla_ref[...].astype(F32) * (gr * _sigmoid(gr))).astype(BF16)
    u_gla = _dot(og, wgla_ref[...])
    merged = _sigmoid(bra_ref[...].astype(F32)) * u_mla + _sigmoid(brb_ref[...].astype(F32)) * u_gla
    mix = _dot(merged.astype(BF16), wout_ref[...]).reshape(bt, lt, d)
    y = alpha * x_ref[...] + (1.0 + gt_ref[...]) * mix
    o_ref[...] = _ln(y) * g_ref[...] + b_ref[...]


def _post(x, mod, o_mla, o_gla, big, wts, cols, alpha, sample):
    b, l, d = x.shape
    bt, lt = _token_tiles(b, l)
    nl = l // lt
    gr0, br0 = cols
    if sample:
        heads, c_lat = o_mla.shape[1], o_mla.shape[3]
        mla_spec = pl.BlockSpec((bt, heads, lt, c_lat), lambda bi, li: (bi, 0, li, 0))
        weights = [wts["w_uv_p"], wts["w_mla_br"], wts["w_gla_br"], wts["w_out"], wts["ln1_g"], wts["ln1_b"]]
    else:
        mla_spec = _rows_spec(bt, lt, nl, o_mla.shape[1])
        weights = [wts["w_mla_br"], wts["w_gla_br"], wts["w_out"], wts["ln1_g"], wts["ln1_b"]]
    in_specs = [_x_spec(bt, lt, d), _mod_spec(bt, d, 2), mla_spec, _rows_spec(bt, lt, nl, o_gla.shape[1]),
                _rows_spec(bt, lt, nl, d, gr0 // d), _rows_spec(bt, lt, nl, d, br0 // d),
                _rows_spec(bt, lt, nl, d, br0 // d + 1)]
    return pl.pallas_call(
        functools.partial(_post_body, sample=sample, alpha=alpha),
        grid=(b // bt, nl),
        in_specs=in_specs + [_resident(w.shape) for w in weights],
        out_specs=_x_spec(bt, lt, d),
        out_shape=jax.ShapeDtypeStruct((b, l, d), F32),
        compiler_params=_cparams(("parallel", "parallel")),
        name="merge_sample" if sample else "merge_prompt",
    )(x, mod, o_mla, o_gla, big, big, big, *weights)


def _mlp_body(x_ref, sc_ref, sh_ref, gt_ref, wup_ref, wdn_ref, g_ref, b_ref, o_ref, *, alpha, chunk):
    bt, lt, d = x_ref.shape
    x = x_ref[...]
    hb = (_ln(x) * (1.0 + sc_ref[...]) + sh_ref[...]).reshape(bt * lt, d).astype(BF16)
    f = jnp.zeros((bt * lt, d), F32)
    for c0 in range(0, wup_ref.shape[1], chunk):
        u = jnp.maximum(_dot(hb, wup_ref[:, c0:c0 + chunk]), 0.0)
        f = f + _dot((u * u).astype(BF16), wdn_ref[c0:c0 + chunk, :])
    y = alpha * x + (1.0 + gt_ref[...]) * f.reshape(bt, lt, d)
    o_ref[...] = _ln(y) * g_ref[...] + b_ref[...]


def _mlp(x, mod, wts, alpha):
    b, l, d = x.shape
    bt, lt = _token_tiles(b, l)
    weights = [wts["w_up"], wts["w_down"], wts["ln2_g"], wts["ln2_b"]]
    return pl.pallas_call(
        functools.partial(_mlp_body, alpha=alpha, chunk=1024),
        grid=(b // bt, l // lt),
        in_specs=[_x_spec(bt, lt, d), _mod_spec(bt, d, 4), _mod_spec(bt, d, 3), _mod_spec(bt, d, 5)]
        + [_resident(w.shape) for w in weights],
        out_specs=_x_spec(bt, lt, d),
        out_shape=jax.ShapeDtypeStruct((b, l, d), F32),
        compiler_params=_cparams(("parallel", "parallel")),
        name="mlp",
    )(x, mod, mod, mod, *weights)


def _rot_cols(w):
    half = w.shape[-1] // 2
    return jnp.concatenate([-w[..., half:], w[..., :half]], axis=-1)


def _pad_cols(w, width):
    return jnp.pad(w, [(0, 0)] * (w.ndim - 1) + [(0, width - w.shape[-1])])


def _rope_tables(pos, rope):
    half = rope // 2
    inv = ROPE_THETA ** (-jnp.arange(half, dtype=F32) / half)
    ang = pos.astype(F32)[:, None] * inv[None, :]
    cos, sin = jnp.cos(ang), jnp.sin(ang)
    n = pos.shape[0]
    cos_t = jnp.concatenate([cos, cos, jnp.ones((n, LANES - 2 * rope), F32), jnp.zeros((n, rope), F32)], axis=1)
    sin_t = jnp.concatenate([sin, sin, jnp.zeros((n, LANES - rope), F32)], axis=1)
    return cos_t, sin_t


def kernel(x_prompt, x_sample, cache_ckv, cache_kpe, state_gla, page_table, c_prompt, c_sample, w_ada, b_ada, w_in, q_norm, kv_norm, w_uq, w_uk, w_uv, w_mla_br, w_gate_up, b_gate, gla_norm, w_gla_br, w_out, ln1_g, ln1_b, w_up, w_down, ln2_g, ln2_b):
    depth, d, _ = w_in.shape
    q_lora, kv_lora = q_norm.shape[1], kv_norm.shape[1]
    rope = cache_kpe.shape[3]
    heads, nope = w_uk.shape[2], w_uk.shape[3]
    v_dim = w_uv.shape[3]
    lowrank = w_gate_up.shape[1]
    dv = gla_norm.shape[1]
    gheads = w_gla_br.shape[1] // dv
    dk = w_gate_up.shape[2] // gheads
    alpha = (2.0 * depth) ** 0.25
    assert rope + nope <= LANES and 2 * v_dim == LANES and heads % 2 == 0 and lowrank <= LANES

    splits = (q_lora, kv_lora, rope, gheads * dk, gheads * dk, gheads * dv, lowrank, gheads * dv, 2 * d)
    offs = [0]
    for s in splits:
        offs.append(offs[-1] + s)
    seg = [w_in[:, :, offs[i]:offs[i + 1]] for i in range(len(splits))]
    w_qdn, w_kvdn, w_kr, w_gq, w_gk, w_gv, w_ga, w_gr, w_br = seg
    small_cols = [w_qdn, w_kvdn, _pad_cols(w_kr, LANES), _pad_cols(_rot_cols(w_kr), LANES), _pad_cols(w_ga, LANES)]
    n_small = sum(w.shape[2] for w in small_cols)
    assert n_small % 512 == 0 and q_lora % LANES == 0 and kv_lora % LANES == 0
    w_in_r = jnp.concatenate(small_cols + [w_gq, w_gk, w_gv, w_gr, w_br], axis=2).astype(BF16)
    q0, k0 = 0, gheads * dk
    v0 = 2 * gheads * dk
    gr0 = v0 + gheads * dv
    br0 = gr0 + gheads * dv

    w_uq_h = w_uq.reshape(depth, q_lora, heads, nope + rope)
    w_q_nope, w_q_rope = w_uq_h[..., :nope], w_uq_h[..., nope:]
    zpad = jnp.zeros((depth, q_lora, heads, LANES - rope - nope), F32)
    w_q_main = jnp.concatenate([w_q_rope, w_q_nope, zpad], axis=-1)
    w_q_rot = jnp.concatenate([_rot_cols(w_q_rope), jnp.zeros_like(w_q_nope), zpad], axis=-1)
    w_uq2 = jnp.concatenate([w_q_main.reshape(depth, q_lora, heads * LANES),
                             w_q_rot.reshape(depth, q_lora, heads * LANES)], axis=-1).astype(BF16)
    w_uk_p = jnp.pad(w_uk, ((0, 0), (0, 0), (0, 0), (rope, LANES - rope - nope)))
    w_ukt_p = jnp.transpose(w_uk_p, (0, 2, 3, 1)).astype(BF16)
    w_uk_p = w_uk_p.reshape(depth, kv_lora, heads * LANES).astype(BF16)
    w_uv_f = _pad_cols(w_uv, LANES).reshape(depth, kv_lora, heads * LANES).astype(BF16)
    eye = jnp.eye(heads, dtype=F32)
    w_uv_p = (w_uv.transpose(0, 2, 1, 3)[:, :, :, None, :] * eye[None, :, None, :, None]).reshape(
        depth, heads, kv_lora, heads * v_dim).astype(BF16)
    w_gate = jnp.pad(w_gate_up, ((0, 0), (0, LANES - lowrank), (0, 0))).astype(BF16)

    def layer_weights(l):
        return dict(
            q_norm=q_norm[l][None], kv_norm=kv_norm[l][None], w_uq2=w_uq2[l], w_uk_p=w_uk_p[l], w_ukt_p=w_ukt_p[l],
            w_uv=w_uv_f[l], w_uv_p=w_uv_p[l], w_gate=w_gate[l], b_gate=b_gate[l][None],
            w_mla_br=w_mla_br[l].astype(BF16), w_gla_br=w_gla_br[l].astype(BF16), w_out=w_out[l].astype(BF16),
            ln1_g=ln1_g[l][None], ln1_b=ln1_b[l][None], w_up=w_up[l].astype(BF16), w_down=w_down[l].astype(BF16),
            ln2_g=ln2_g[l][None], ln2_b=ln2_b[l][None])

    bp, lp, _ = x_prompt.shape
    bs, ls, _ = x_sample.shape
    past_len = page_table.shape[1] * cache_ckv.shape[2]
    mod_all = _ada(jnp.concatenate([c_prompt, c_sample], axis=0), w_ada, b_ada)
    tabs_p = _rope_tables(jnp.arange(lp, dtype=jnp.int32), rope)
    tabs_s = _rope_tables(past_len + jnp.arange(ls, dtype=jnp.int32), rope)
    dims = (heads, q_lora, kv_lora, rope, nope)
    cache_kpe_t = jnp.swapaxes(cache_kpe, 2, 3)

    xp, xs = x_prompt, x_sample
    outs = {k: [] for k in ("ckv_p", "kpe_p", "gla_p", "ckv_s", "kpe_s", "gla_s")}
    for l in range(depth):
        wts = layer_weights(l)
        gn = gla_norm[l][None]
        mod = mod_all[l, :bp][:, None, :]
        small, big = _inproj(xp, mod, w_in_r[l], n_small)
        ckv, kpe, log_a, q, k, v = _prep(small, *tabs_p, bp, lp, wts, dims, sample=False)
        o_mla = _flash(q, k, v, bp, lp, heads)
        o_gla, s_new = _gla(big, log_a, gn, None, bp, lp, gheads, dk, dv, (q0, k0, v0))
        x1 = _post(xp, mod, o_mla, o_gla, big, wts, (gr0, br0), alpha, sample=False)
        xp = _mlp(x1, mod, wts, alpha)
        outs["ckv_p"].append(ckv.reshape(bp, lp, kv_lora))
        outs["kpe_p"].append(kpe.reshape(bp, lp, rope))
        outs["gla_p"].append(s_new)
        mod = mod_all[l, bp:][:, None, :]
        small, big = _inproj(xs, mod, w_in_r[l], n_small)
        ckv, kpe, log_a, qlat, qpe = _prep(small, *tabs_s, bs, ls, wts, dims, sample=True)
        o_lat = _decode(qlat, qpe, ckv, kpe, cache_ckv, cache_kpe_t, page_table, l)
        o_gla, s_new = _gla(big, log_a, gn, (state_gla, l), bs, ls, gheads, dk, dv, (q0, k0, v0))
        x1 = _post(xs, mod, o_lat, o_gla, big, wts, (gr0, br0), alpha, sample=True)
        xs = _mlp(x1, mod, wts, alpha)
        outs["ckv_s"].append(ckv.reshape(bs, ls, kv_lora))
        outs["kpe_s"].append(kpe.reshape(bs, ls, rope))
        outs["gla_s"].append(s_new)

    st = {k: jnp.stack(v) for k, v in outs.items()}
    return (xp, xs, st["ckv_p"], st["kpe_p"], st["gla_p"], st["ckv_s"], st["kpe_s"], st["gla_s"])
```

```python
import functools

import jax
import jax.numpy as jnp
from jax import lax
from jax.experimental import pallas as pl
from jax.experimental.pallas import tpu as pltpu

F32 = jnp.float32
BF16 = jnp.bfloat16

LANES = 128
SUBLANES = 8
ROPE_THETA = 10000.0
GLA_TAU = 16.0
LOG2_E = 1.4426950408889634
GLA_CHUNK = 128
GLA_SHORT_CHUNK = 64
EPS = 1e-6
N_ADA = 6
ROW_TILE = 512
ATTN_TILE = 1024
PAGES_PER_CHUNK = 32
VMEM_LIMIT = 56 * 1024 * 1024


def _cparams(sem):
    return pltpu.CompilerParams(dimension_semantics=sem, vmem_limit_bytes=VMEM_LIMIT)


def _resident(shape):
    nd = len(shape)
    return pl.BlockSpec(shape, lambda *_: (0,) * nd, pipeline_mode=pl.Buffered(1))


def _dot(a, b):
    return jnp.dot(a, b, preferred_element_type=F32)


def _dot_nt(a, b):
    return lax.dot_general(a, b, (((1,), (1,)), ((), ())), preferred_element_type=F32)


def _ln(x):
    mu = jnp.mean(x, axis=-1, keepdims=True)
    xc = x - mu
    var = jnp.mean(xc * xc, axis=-1, keepdims=True)
    return xc * lax.rsqrt(var + EPS)


def _rms(x):
    return x * lax.rsqrt(jnp.mean(x * x, axis=-1, keepdims=True) + EPS)


def _sigmoid(x):
    return 1.0 / (1.0 + jnp.exp(-x))


def _log_sigmoid(x):
    return jnp.minimum(x, 0.0) - jnp.log(1.0 + jnp.exp(-jnp.abs(x)))


def _ada_body(c_ref, w_ref, b_ref, o_ref):
    c = c_ref[...]
    s = (c * _sigmoid(c)).astype(BF16)
    o_ref[0] = _dot(s, w_ref[0].astype(BF16)) + b_ref[0]


def _ada(c_all, w_ada, b_ada):
    depth, d, n = w_ada.shape
    rows = c_all.shape[0]
    tn = 1536
    return pl.pallas_call(
        _ada_body,
        grid=(depth, n // tn),
        in_specs=[
            pl.BlockSpec((rows, d), lambda l, j: (0, 0)),
            pl.BlockSpec((1, d, tn), lambda l, j: (l, 0, j)),
            pl.BlockSpec((1, 1, tn), lambda l, j: (l, 0, j)),
        ],
        out_specs=pl.BlockSpec((1, rows, tn), lambda l, j: (l, 0, j)),
        out_shape=jax.ShapeDtypeStruct((depth, rows, n), F32),
        compiler_params=_cparams(("arbitrary", "arbitrary")),
        name="ada_mod",
    )(c_all, w_ada, b_ada.reshape(depth, 1, n))


def _token_tiles(b, l):
    if l >= ROW_TILE:
        assert l % ROW_TILE == 0
        return 1, ROW_TILE
    bt = min(b, ROW_TILE // l)
    assert b % bt == 0
    return bt, l


def _x_spec(bt, lt, d):
    return pl.BlockSpec((bt, lt, d), lambda b, l: (b, l, 0))


def _mod_spec(bt, d, k):
    return pl.BlockSpec((bt, 1, d), lambda b, l: (b, 0, k))


def _rows_spec(bt, lt, nl, n, col=0):
    return pl.BlockSpec((bt * lt, n), lambda b, l: (b * nl + l, col))


def _inproj_body(x_ref, sc_ref, sh_ref, w_ref, small_ref, big_ref, *, n_small, chunk):
    bt, lt, d = x_ref.shape
    h = _ln(x_ref[...]) * (1.0 + sc_ref[...]) + sh_ref[...]
    hb = h.reshape(bt * lt, d).astype(BF16)
    for c0 in range(0, w_ref.shape[1], chunk):
        acc = _dot(hb, w_ref[:, c0:c0 + chunk])
        if c0 < n_small:
            small_ref[:, c0:c0 + chunk] = acc
        else:
            big_ref[:, c0 - n_small:c0 - n_small + chunk] = acc.astype(BF16)


def _inproj(x, mod, w_in_r, n_small):
    b, l, d = x.shape
    bt, lt = _token_tiles(b, l)
    nl = l // lt
    n_big = w_in_r.shape[1] - n_small
    return pl.pallas_call(
        functools.partial(_inproj_body, n_small=n_small, chunk=512),
        grid=(b // bt, nl),
        in_specs=[_x_spec(bt, lt, d), _mod_spec(bt, d, 1), _mod_spec(bt, d, 0), _resident(w_in_r.shape)],
        out_specs=[_rows_spec(bt, lt, nl, n_small), _rows_spec(bt, lt, nl, n_big)],
        out_shape=[jax.ShapeDtypeStruct((b * l, n_small), F32), jax.ShapeDtypeStruct((b * l, n_big), BF16)],
        compiler_params=_cparams(("parallel", "parallel")),
        name="in_proj",
    )(x, mod, mod, w_in_r)


def _prep_common(small_ref, cos_ref, sin_ref, qn_ref, wuq_ref, kvn_ref, wg_ref, bg_ref,
                 ckv_ref, kpe_ref, la_ref, *, q_lora, kv_lora, rope):
    rows = small_ref.shape[0]
    lt = cos_ref.shape[0]
    s = small_ref[...]
    o = q_lora + kv_lora
    q_dn, kv_dn = s[:, :q_lora], s[:, q_lora:o]
    kr, krr, ga = s[:, o:o + LANES], s[:, o + LANES:o + 2 * LANES], s[:, o + 2 * LANES:o + 3 * LANES]
    cos, sin = cos_ref[...], sin_ref[...]
    if rows != lt:
        cos = jnp.broadcast_to(cos[None], (rows // lt, lt, LANES)).reshape(rows, LANES)
        sin = jnp.broadcast_to(sin[None], (rows // lt, lt, LANES)).reshape(rows, LANES)
    qn = (_rms(q_dn) * qn_ref[...]).astype(BF16)
    q2 = _dot(qn, wuq_ref[...])
    ckv = _rms(kv_dn) * kvn_ref[...]
    ckv_ref[...] = ckv
    kpe = kr * cos + krr * sin
    kpe_ref[...] = kpe[:, :rope]
    xg = _dot(ga.astype(BF16), wg_ref[...]) + bg_ref[...]
    la_ref[...] = _log_sigmoid(xg) * (1.0 / GLA_TAU)
    return q2, ckv, kpe, cos, sin


def _prep_prompt_body(small_ref, cos_ref, sin_ref, qn_ref, wuq_ref, kvn_ref, wg_ref, bg_ref, wuk_ref, wuv_ref,
                      ckv_all_ref, ckv_ref, kpe_ref, la_ref, q_ref, k_ref, v_ref, *, heads, scale, **kw):
    del ckv_all_ref
    q2, ckv, kpe, cos, sin = _prep_common(small_ref, cos_ref, sin_ref, qn_ref, wuq_ref, kvn_ref, wg_ref, bg_ref,
                                          ckv_ref, kpe_ref, la_ref, **kw)
    cb = ckv.astype(BF16)
    kn = _dot(cb, wuk_ref[...])
    hw = heads * LANES
    v = _dot(cb, wuv_ref[...])
    upper = lax.broadcasted_iota(jnp.int32, v.shape, 1) % LANES >= LANES // 2
    v_ref[...] = jnp.where(upper, 1.0, v).astype(BF16)
    for h in range(heads):
        c0 = h * LANES
        qh = (q2[:, c0:c0 + LANES] * cos + q2[:, hw + c0:hw + c0 + LANES] * sin) * scale
        q_ref[:, c0:c0 + LANES] = qh.astype(BF16)
        k_ref[:, c0:c0 + LANES] = (kn[:, c0:c0 + LANES] + kpe).astype(BF16)


def _prep_sample_body(small_ref, cos_ref, sin_ref, qn_ref, wuq_ref, kvn_ref, wg_ref, bg_ref, wukt_ref,
                      ckv_all_ref, ckv_ref, kpe_ref, la_ref, qlat_ref, qpe_ref, *, heads, scale, rope, **kw):
    del ckv_all_ref
    q2, _, _, cos, sin = _prep_common(small_ref, cos_ref, sin_ref, qn_ref, wuq_ref, kvn_ref, wg_ref, bg_ref,
                                      ckv_ref, kpe_ref, la_ref, rope=rope, **kw)
    bt, _, lt, c_lat = qlat_ref.shape
    hw = heads * LANES
    for h in range(heads):
        c0 = h * LANES
        qh = (q2[:, c0:c0 + LANES] * cos + q2[:, hw + c0:hw + c0 + LANES] * sin) * scale
        qlat = _dot(qh.astype(BF16), wukt_ref[h])
        qlat_ref[:, h] = qlat.reshape(bt, lt, c_lat)
        qpe_ref[:, h] = qh[:, :rope].reshape(bt, lt, rope)


def _prep(small, cos_t, sin_t, b, l, wts, dims, sample, ckv_all, layer):
    heads, q_lora, kv_lora, rope, nope = dims
    bt, lt = _token_tiles(b, l)
    nl = l // lt
    rows = bt * lt
    t = b * l
    scale = float(nope + rope) ** -0.5 * (1.0 if sample else LOG2_E)
    tab = pl.BlockSpec((lt, LANES), lambda bi, li: (li, 0))
    common_w = [wts["q_norm"], wts["w_uq2"], wts["kv_norm"], wts["w_gate"], wts["b_gate"]]
    common_out_specs = [pl.BlockSpec((None, rows, kv_lora), lambda bi, li: (layer, bi * nl + li, 0)),
                        _rows_spec(bt, lt, nl, rope), _rows_spec(bt, lt, nl, wts["w_gate"].shape[1])]
    common_out_shape = [jax.ShapeDtypeStruct(ckv_all.shape, F32), jax.ShapeDtypeStruct((t, rope), F32),
                        jax.ShapeDtypeStruct((t, wts["w_gate"].shape[1]), F32)]
    kw = dict(heads=heads, scale=scale, q_lora=q_lora, kv_lora=kv_lora, rope=rope)
    if not sample:
        extra_w = [wts["w_uk_p"], wts["w_uv"]]
        hw = heads * LANES
        nv = wts["w_uv"].shape[1]
        out_specs = common_out_specs + [_rows_spec(bt, lt, nl, hw), _rows_spec(bt, lt, nl, hw),
                                        _rows_spec(bt, lt, nl, nv)]
        out_shape = common_out_shape + [jax.ShapeDtypeStruct((t, hw), BF16), jax.ShapeDtypeStruct((t, hw), BF16),
                                        jax.ShapeDtypeStruct((t, nv), BF16)]
        body = functools.partial(_prep_prompt_body, **kw)
    else:
        extra_w = [wts["w_ukt_p"]]
        out_specs = common_out_specs + [
            pl.BlockSpec((bt, heads, lt, kv_lora), lambda bi, li: (bi, 0, li, 0)),
            pl.BlockSpec((bt, heads, lt, rope), lambda bi, li: (bi, 0, li, 0))]
        out_shape = common_out_shape + [jax.ShapeDtypeStruct((b, heads, l, kv_lora), F32),
                                        jax.ShapeDtypeStruct((b, heads, l, rope), F32)]
        body = functools.partial(_prep_sample_body, **kw)
    weights = common_w + extra_w
    return pl.pallas_call(
        body,
        grid=(b // bt, nl),
        in_specs=[_rows_spec(bt, lt, nl, small.shape[1]), tab, tab] + [_resident(w.shape) for w in weights]
        + [pl.BlockSpec(memory_space=pl.ANY)],
        out_specs=out_specs,
        out_shape=out_shape,
        input_output_aliases={3 + len(weights): 0},
        compiler_params=_cparams(("parallel", "parallel")),
        name="prep_sample" if sample else "prep_prompt",
    )(small, cos_t, sin_t, *weights, ckv_all)


def _flash_body(qi_ref, kj_ref, q_ref, k_ref, v_ref, o_ref, m_ref, acc_ref, *, heads):
    i, j = qi_ref[pl.program_id(1)], kj_ref[pl.program_id(1)]
    tq, tk = q_ref.shape[1], k_ref.shape[1]
    half = LANES // 2

    @pl.when(j == 0)
    def _():
        m_ref[...] = jnp.full(m_ref.shape, -jnp.inf, F32)
        acc_ref[...] = jnp.zeros(acc_ref.shape, F32)

    def update(h, r0, nr, nc, keep):
        c0 = h * LANES
        s = _dot_nt(q_ref[0, r0:r0 + nr, c0:c0 + LANES], k_ref[0, 0:nc, c0:c0 + LANES])
        if keep is not None:
            s = jnp.where(keep, s, -jnp.inf)
        m_prev = m_ref[h, r0:r0 + nr]
        m_new = jnp.maximum(m_prev, jnp.max(s, axis=-1, keepdims=True))
        p = jnp.exp2(s - jnp.tile(m_new, (1, nc // LANES)))
        alpha = jnp.exp2(m_prev - m_new)
        acc_ref[h, r0:r0 + nr] = alpha * acc_ref[h, r0:r0 + nr] + _dot(p.astype(BF16), v_ref[0, 0:nc, c0:c0 + LANES])
        m_ref[h, r0:r0 + nr] = m_new

    def step(masked):
        keep = None
        if masked:
            keep = lax.broadcasted_iota(jnp.int32, (tq, tk), 0) >= lax.broadcasted_iota(jnp.int32, (tq, tk), 1)
        for h in range(heads):
            update(h, 0, tq, tk, keep)

    @pl.when(j < i)
    def _():
        step(False)

    @pl.when(j == i)
    def _():
        step(True)
        low = lax.broadcasted_iota(jnp.int32, (tq, LANES), 1) < half
        for hp in range(heads // 2):
            a, b = acc_ref[2 * hp], acc_ref[2 * hp + 1]
            oa = a / pltpu.roll(a, half, 1)
            ob = b / pltpu.roll(b, half, 1)
            o_ref[0, :, hp * LANES:(hp + 1) * LANES] = jnp.where(low, oa, pltpu.roll(ob, half, 1)).astype(BF16)


def _flash(q, k, v, b, l, heads):
    t = min(ATTN_TILE, l)
    n = l // t
    hw = heads * LANES
    nv = hw // 2
    q3, k3, v3 = q.reshape(b, l, hw), k.reshape(b, l, hw), v.reshape(b, l, hw)
    pairs = [(i, j) for i in range(n) for j in range(i + 1)]
    qi = jnp.asarray([p[0] for p in pairs], jnp.int32)
    kj = jnp.asarray([p[1] for p in pairs], jnp.int32)
    grid_spec = pltpu.PrefetchScalarGridSpec(
        num_scalar_prefetch=2,
        grid=(b, len(pairs)),
        in_specs=[
            pl.BlockSpec((1, t, hw), lambda bi, s, qi_, kj_: (bi, qi_[s], 0)),
            pl.BlockSpec((1, t, hw), lambda bi, s, qi_, kj_: (bi, kj_[s], 0)),
            pl.BlockSpec((1, t, hw), lambda bi, s, qi_, kj_: (bi, kj_[s], 0)),
        ],
        out_specs=pl.BlockSpec((1, t, nv), lambda bi, s, qi_, kj_: (bi, qi_[s], 0)),
        scratch_shapes=[pltpu.VMEM((heads, t, LANES), F32), pltpu.VMEM((heads, t, LANES), F32)],
    )
    out = pl.pallas_call(
        functools.partial(_flash_body, heads=heads),
        grid_spec=grid_spec,
        out_shape=jax.ShapeDtypeStruct((b, l, nv), BF16),
        compiler_params=_cparams(("parallel", "arbitrary")),
        name="flash_prompt",
    )(qi, kj, q3, k3, v3)
    return out.reshape(b * l, nv)


def _decode_body(pt_ref, qlat_ref, qpe_ref, ckvn_ref, kpen_ref, ckv_hbm, kpet_hbm, o_ref,
                 kbuf, pbuf, kb16, sbuf, kn_ref, pn_ref, sem, *, layer, n_pages, chunk, page):
    b, nb = pl.program_id(0), pl.num_programs(0)
    slot, nslot = b % 2, 1 - b % 2
    _, heads, lq, c_lat = qlat_ref.shape
    rope = qpe_ref.shape[3]
    rows = heads * lq
    n_chunks = n_pages // chunk
    width = chunk * page
    ql = qlat_ref[0].reshape(rows, c_lat).astype(BF16)
    qp = qpe_ref[0].reshape(rows, rope).astype(BF16)

    def page_copies(phys, ci, p, sl):
        r0 = pl.multiple_of((ci * chunk + p) * page, page)
        return (pltpu.make_async_copy(ckv_hbm.at[layer, phys], kbuf.at[sl, pl.ds(r0, page), :], sem.at[sl, 0, ci]),
                pltpu.make_async_copy(kpet_hbm.at[layer, phys], pbuf.at[sl, :, pl.ds(r0, page)], sem.at[sl, 1, ci]))

    @pl.when(b == 0)
    def _():
        def issue(ci, carry):
            for p in range(chunk):
                for cp in page_copies(pt_ref[ci * chunk + p], ci, p, 0):
                    cp.start()
            return carry
        lax.fori_loop(0, n_chunks, issue, 0)

    kn_ref[...] = jnp.zeros(kn_ref.shape, F32)
    pn_ref[...] = jnp.zeros(pn_ref.shape, F32)
    kn_ref[0:lq, :] = ckvn_ref[...]
    pn_ref[0:lq, :] = kpen_ref[...]
    kn = kn_ref[...].astype(BF16)
    s_new = _dot_nt(ql, kn) + _dot_nt(qp, pn_ref[...].astype(BF16))
    key = lax.broadcasted_iota(jnp.int32, (rows, LANES), 1)
    qpos = lax.broadcasted_iota(jnp.int32, (rows, LANES), 0) % lq
    s_new = jnp.where(key <= qpos, s_new, -jnp.inf)

    nxt = jnp.minimum(b + 1, nb - 1)

    def scores(i, m):
        for p in range(chunk):
            for cp in page_copies(0, i, p, slot):
                cp.wait()
        phys = [pt_ref[nxt * n_pages + i * chunk + p] for p in range(chunk)]
        for p in range(chunk):
            for cp in page_copies(phys[p], i, p, nslot):
                cp.start()
        c0 = pl.multiple_of(i * width, width)
        kb = kbuf[slot, pl.ds(c0, width), :].astype(BF16)
        kb16[pl.ds(c0, width), :] = kb
        s = _dot_nt(ql, kb) + _dot(qp, pbuf[slot, :, pl.ds(c0, width)].astype(BF16))
        sbuf[:, pl.ds(c0, width)] = s
        for t in range(width // LANES):
            m = jnp.maximum(m, s[:, t * LANES:(t + 1) * LANES])
        return m

    m = lax.fori_loop(0, n_chunks, scores, s_new)
    m = jnp.broadcast_to(jnp.max(m, axis=-1, keepdims=True), (rows, LANES))
    p_new = jnp.exp(s_new - m)
    m_wide = jnp.tile(m, (1, width // LANES))

    def values(i, carry):
        l, acc = carry
        c0 = pl.multiple_of(i * width, width)
        p = jnp.exp(sbuf[:, pl.ds(c0, width)] - m_wide)
        for t in range(width // LANES):
            l = l + p[:, t * LANES:(t + 1) * LANES]
        return l, acc + _dot(p.astype(BF16), kb16[pl.ds(c0, width), :])

    l, acc = lax.fori_loop(0, n_chunks, values, (p_new, _dot(p_new.astype(BF16), kn)))
    o = acc / jnp.sum(l, axis=-1, keepdims=True)
    o_ref[0] = o.reshape(heads, lq, c_lat)

    @pl.when(b == nb - 1)
    def _():
        def drain(ci, carry):
            for p in range(chunk):
                for cp in page_copies(0, ci, p, nslot):
                    cp.wait()
            return carry
        lax.fori_loop(0, n_chunks, drain, 0)


def _decode(qlat, qpe, ckv_new, kpe_new, cache_ckv, cache_kpe_t, page_table, layer):
    b, heads, lq, c_lat = qlat.shape
    rope = qpe.shape[3]
    n_pages = page_table.shape[1]
    page = cache_ckv.shape[2]
    chunk = min(PAGES_PER_CHUNK, n_pages)
    assert n_pages % chunk == 0 and lq % SUBLANES == 0 and lq <= LANES and page % LANES == 0
    rows = heads * lq
    t_past = n_pages * page
    any_spec = pl.BlockSpec(memory_space=pl.ANY)
    grid_spec = pltpu.PrefetchScalarGridSpec(
        num_scalar_prefetch=1,
        grid=(b,),
        in_specs=[
            pl.BlockSpec((1, heads, lq, c_lat), lambda bi, pt: (bi, 0, 0, 0)),
            pl.BlockSpec((1, heads, lq, rope), lambda bi, pt: (bi, 0, 0, 0)),
            pl.BlockSpec((None, lq, c_lat), lambda bi, pt: (layer, bi, 0)),
            pl.BlockSpec((lq, rope), lambda bi, pt: (bi, 0)),
            any_spec, any_spec,
        ],
        out_specs=pl.BlockSpec((1, heads, lq, c_lat), lambda bi, pt: (bi, 0, 0, 0)),
        scratch_shapes=[pltpu.VMEM((2, t_past, c_lat), F32), pltpu.VMEM((2, rope, t_past), F32),
                        pltpu.VMEM((t_past, c_lat), BF16), pltpu.VMEM((rows, t_past), F32),
                        pltpu.VMEM((LANES, c_lat), F32), pltpu.VMEM((LANES, rope), F32),
                        pltpu.SemaphoreType.DMA((2, 2, n_pages // chunk))],
    )
    return pl.pallas_call(
        functools.partial(_decode_body, layer=layer, n_pages=n_pages, chunk=chunk, page=page),
        grid_spec=grid_spec,
        out_shape=jax.ShapeDtypeStruct((b, heads, lq, c_lat), F32),
        compiler_params=_cparams(("arbitrary",)),
        name="decode_sample",
    )(page_table.reshape(-1), qlat, qpe, ckv_new, kpe_new, cache_ckv, cache_kpe_t)


def _gla_masks(c, seg):
    row = lax.broadcasted_iota(jnp.int32, (c, c), 0)
    col = lax.broadcasted_iota(jnp.int32, (c, c), 1)
    tri = ((row >= col) & (row // seg == col // seg)).astype(BF16)
    levels = []
    m = seg // 2
    while m >= SUBLANES:
        levels.append((m, (row // (2 * m) == col // (2 * m)) & (row % (2 * m) >= m) & (col % (2 * m) < m)))
        m //= 2
    shape3 = (c // SUBLANES, SUBLANES, c)
    sub = lax.broadcasted_iota(jnp.int32, shape3, 0)
    trow = lax.broadcasted_iota(jnp.int32, shape3, 1)
    col3 = lax.broadcasted_iota(jnp.int32, shape3, 2)
    diag = [(col3 == sub * SUBLANES + s) & (trow >= s) for s in range(SUBLANES)]
    return tri, levels, diag


def _gla_chunk_intra(q, k, v, g, masks):
    tri, levels, diag = masks
    c, dk = q.shape
    g_hi = g.astype(BF16)
    g_lo = (g - g_hi.astype(F32)).astype(BF16)
    b = _dot(tri, g_hi) + _dot(tri, g_lo)

    a = jnp.zeros((c, c), F32)
    for m, valid in levels:
        b3 = b.reshape(c // (2 * m), 2 * m, dk)
        ref = jnp.broadcast_to(b3[:, m - 1:m, :], b3.shape).reshape(c, dk)
        f = jnp.exp2(-jnp.abs(b - ref))
        pm = _dot_nt((q * f).astype(BF16), (k * f).astype(BF16))
        a = jnp.where(valid, pm, a)

    nsub = c // SUBLANES
    q3, k3, b3 = (x.reshape(nsub, SUBLANES, dk) for x in (q, k, b))
    a3 = a.reshape(nsub, SUBLANES, c)
    for s in range(SUBLANES):
        bs = jnp.broadcast_to(b3[:, s:s + 1, :], b3.shape)
        ks = jnp.broadcast_to(k3[:, s:s + 1, :], k3.shape)
        w = q3 * ks * jnp.exp2(jnp.minimum(b3 - bs, 0.0))
        a3 = jnp.where(diag[s], jnp.sum(w, axis=-1, keepdims=True), a3)
    return b, _dot(a3.reshape(c, c).astype(BF16), v.astype(BF16))


def _gla_body(*refs, c, nb, tl, seg, use_s0, scale):
    q_ref, k_ref, v_ref, g_ref, gn_ref = refs[:5]
    s0_ref = refs[5] if use_s0 else None
    o_ref, sf_ref, st_ref = refs[-3:]
    li = pl.program_id(2)
    dk, dv = q_ref.shape[1], v_ref.shape[1]

    @pl.when(li == 0)
    def _():
        for ib in range(nb):
            st_ref[ib] = s0_ref[ib, 0].T if use_s0 else jnp.zeros((dv, dk), F32)

    gn = gn_ref[...]
    n_chunks = nb * tl // c
    nseg = c // seg
    rowseg = lax.broadcasted_iota(jnp.int32, (c, 1), 0) // seg
    masks = _gla_masks(c, seg)
    for ci in range(n_chunks):
        r0 = ci * c
        q = q_ref[r0:r0 + c, :].astype(F32) * scale
        k = k_ref[r0:r0 + c, :].astype(F32)
        v = v_ref[r0:r0 + c, :].astype(F32)
        g = g_ref[r0:r0 + c, :] * LOG2_E
        b, o = _gla_chunk_intra(q, k, v, g, masks)
        b3 = b.reshape(nseg, seg, dk)
        b_last = jnp.broadcast_to(b3[:, seg - 1:seg, :], b3.shape).reshape(c, dk)
        qe = (q * jnp.exp2(b)).astype(BF16)
        ke = (k * jnp.exp2(b_last - b)).astype(BF16)
        vt = v.T.astype(BF16)
        for sg in range(nseg):
            ib = (ci * nseg + sg) if nseg > 1 else (ci * c) // tl
            st = st_ref[ib]
            if nseg > 1:
                own = rowseg == sg
                o = o + jnp.where(own, _dot_nt(qe, st.astype(BF16)), 0.0)
                ke_s = jnp.where(own, ke, jnp.zeros_like(ke))
            else:
                o = o + _dot_nt(qe, st.astype(BF16))
                ke_s = ke
            dec = jnp.exp2(b[(sg + 1) * seg - 1:(sg + 1) * seg, :])
            st_ref[ib] = st * dec + _dot(vt, ke_s)
        o_ref[r0:r0 + c, :] = (_rms(o) * gn).astype(BF16)

    @pl.when(li == pl.num_programs(2) - 1)
    def _():
        for ib in range(nb):
            sf_ref[ib, 0] = st_ref[ib].T


def _gla(big, log_a, gla_norm, s0_all, s_out_all, layer, b, l, heads, dk, dv, cols):
    c = GLA_CHUNK if l >= GLA_CHUNK else GLA_SHORT_CHUNK
    if l >= c:
        assert l % c == 0
        nb, seg = 1, c
        tl = min(ROW_TILE, l)
    else:
        assert c % l == 0 and l % SUBLANES == 0
        nb, seg, tl = c // l, l, l
        assert b % nb == 0
    nl = l // tl
    rows = nb * tl
    q0, k0, v0 = cols

    def tok(width, base):
        return pl.BlockSpec((rows, width), lambda bi, h, li: (bi * nl + li, base // width + h))

    in_specs = [tok(dk, q0), tok(dk, k0), tok(dv, v0), tok(dk, 0), pl.BlockSpec((1, dv), lambda bi, h, li: (0, 0))]
    args = [big, big, big, log_a, gla_norm]
    state_spec = pl.BlockSpec((None, nb, 1, dk, dv), lambda bi, h, li: (layer, bi, h, 0, 0))
    if s0_all is not None:
        in_specs.append(state_spec)
        args.append(s0_all)
    in_specs.append(pl.BlockSpec(memory_space=pl.ANY))
    args.append(s_out_all)
    return pl.pallas_call(
        functools.partial(_gla_body, c=c, nb=nb, tl=tl, seg=seg, use_s0=s0_all is not None, scale=float(dk) ** -0.5),
        grid=(b // nb, heads, nl),
        in_specs=in_specs,
        out_specs=[pl.BlockSpec((rows, dv), lambda bi, h, li: (bi * nl + li, h)), state_spec],
        out_shape=[jax.ShapeDtypeStruct((b * l, heads * dv), BF16), jax.ShapeDtypeStruct(s_out_all.shape, F32)],
        input_output_aliases={len(args) - 1: 1},
        scratch_shapes=[pltpu.VMEM((nb, dv, dk), F32)],
        compiler_params=_cparams(("parallel", "parallel", "arbitrary")),
        name="gla",
    )(*args)


def _post_body(*refs, sample, alpha):
    if sample:
        (x_ref, gt_ref, olat_ref, ogla_ref, gr_ref, bra_ref, brb_ref, wuv_ref,
         wmla_ref, wgla_ref, wout_ref, g_ref, b_ref, o_ref) = refs
        bt, heads, lt, c_lat = olat_ref.shape
        o_mla = _dot(olat_ref[:, 0].reshape(bt * lt, c_lat).astype(BF16), wuv_ref[0])
        for h in range(1, heads):
            o_mla = o_mla + _dot(olat_ref[:, h].reshape(bt * lt, c_lat).astype(BF16), wuv_ref[h])
        o_mla = o_mla.astype(BF16)
    else:
        (x_ref, gt_ref, omla_ref, ogla_ref, gr_ref, bra_ref, brb_ref,
         wmla_ref, wgla_ref, wout_ref, g_ref, b_ref, o_ref) = refs
        o_mla = omla_ref[...]
    bt, lt, d = x_ref.shape
    u_mla = _dot(o_mla, wmla_ref[...])
    gr = gr_ref[...].astype(F32)
    og = (ogla_ref[...].astype(F32) * (gr * _sigmoid(gr))).astype(BF16)
    u_gla = _dot(og, wgla_ref[...])
    merged = _sigmoid(bra_ref[...].astype(F32)) * u_mla + _sigmoid(brb_ref[...].astype(F32)) * u_gla
    mix = _dot(merged.astype(BF16), wout_ref[...]).reshape(bt, lt, d)
    y = alpha * x_ref[...] + (1.0 + gt_ref[...]) * mix
    o_ref[...] = _ln(y) * g_ref[...] + b_ref[...]


def _post(x, mod, o_mla, o_gla, big, wts, cols, alpha, sample):
    b, l, d = x.shape
    bt, lt = _token_tiles(b, l)
    nl = l // lt
    gr0, br0 = cols
    if sample:
        heads, c_lat = o_mla.shape[1], o_mla.shape[3]
        mla_spec = pl.BlockSpec((bt, heads, lt, c_lat), lambda bi, li: (bi, 0, li, 0))
        weights = [wts["w_uv_p"], wts["w_mla_br"], wts["w_gla_br"], wts["w_out"], wts["ln1_g"], wts["ln1_b"]]
    else:
        mla_spec = _rows_spec(bt, lt, nl, o_mla.shape[1])
        weights = [wts["w_mla_br"], wts["w_gla_br"], wts["w_out"], wts["ln1_g"], wts["ln1_b"]]
    in_specs = [_x_spec(bt, lt, d), _mod_spec(bt, d, 2), mla_spec, _rows_spec(bt, lt, nl, o_gla.shape[1]),
                _rows_spec(bt, lt, nl, d, gr0 // d), _rows_spec(bt, lt, nl, d, br0 // d),
                _rows_spec(bt, lt, nl, d, br0 // d + 1)]
    return pl.pallas_call(
        functools.partial(_post_body, sample=sample, alpha=alpha),
        grid=(b // bt, nl),
        in_specs=in_specs + [_resident(w.shape) for w in weights],
        out_specs=_x_spec(bt, lt, d),
        out_shape=jax.ShapeDtypeStruct((b, l, d), F32),
        compiler_params=_cparams(("parallel", "parallel")),
        name="merge_sample" if sample else "merge_prompt",
    )(x, mod, o_mla, o_gla, big, big, big, *weights)


def _mlp_body(x_ref, sc_ref, sh_ref, gt_ref, wup_ref, wdn_ref, g_ref, b_ref, o_ref, *, alpha, chunk):
    bt, lt, d = x_ref.shape
    x = x_ref[...]
    hb = (_ln(x) * (1.0 + sc_ref[...]) + sh_ref[...]).reshape(bt * lt, d).astype(BF16)
    f = jnp.zeros((bt * lt, d), F32)
    for c0 in range(0, wup_ref.shape[1], chunk):
        u = jnp.maximum(_dot(hb, wup_ref[:, c0:c0 + chunk]), 0.0)
        f = f + _dot((u * u).astype(BF16), wdn_ref[c0:c0 + chunk, :])
    y = alpha * x + (1.0 + gt_ref[...]) * f.reshape(bt, lt, d)
    o_ref[...] = _ln(y) * g_ref[...] + b_ref[...]


def _mlp(x, mod, wts, alpha):
    b, l, d = x.shape
    bt, lt = _token_tiles(b, l)
    weights = [wts["w_up"], wts["w_down"], wts["ln2_g"], wts["ln2_b"]]
    return pl.pallas_call(
        functools.partial(_mlp_body, alpha=alpha, chunk=1024),
        grid=(b // bt, l // lt),
        in_specs=[_x_spec(bt, lt, d), _mod_spec(bt, d, 4), _mod_spec(bt, d, 3), _mod_spec(bt, d, 5)]
        + [_resident(w.shape) for w in weights],
        out_specs=_x_spec(bt, lt, d),
        out_shape=jax.ShapeDtypeStruct((b, l, d), F32),
        compiler_params=_cparams(("parallel", "parallel")),
        name="mlp",
    )(x, mod, mod, mod, *weights)


def _rot_cols(w):
    half = w.shape[-1] // 2
    return jnp.concatenate([-w[..., half:], w[..., :half]], axis=-1)


def _pad_cols(w, width):
    return jnp.pad(w, [(0, 0)] * (w.ndim - 1) + [(0, width - w.shape[-1])])


def _rope_tables(pos, rope):
    half = rope // 2
    inv = ROPE_THETA ** (-jnp.arange(half, dtype=F32) / half)
    ang = pos.astype(F32)[:, None] * inv[None, :]
    cos, sin = jnp.cos(ang), jnp.sin(ang)
    n = pos.shape[0]
    cos_t = jnp.concatenate([cos, cos, jnp.ones((n, LANES - 2 * rope), F32), jnp.zeros((n, rope), F32)], axis=1)
    sin_t = jnp.concatenate([sin, sin, jnp.zeros((n, LANES - rope), F32)], axis=1)
    return cos_t, sin_t


def kernel(x_prompt, x_sample, cache_ckv, cache_kpe, state_gla, page_table, c_prompt, c_sample, w_ada, b_ada, w_in, q_norm, kv_norm, w_uq, w_uk, w_uv, w_mla_br, w_gate_up, b_gate, gla_norm, w_gla_br, w_out, ln1_g, ln1_b, w_up, w_down, ln2_g, ln2_b):
    depth, d, _ = w_in.shape
    q_lora, kv_lora = q_norm.shape[1], kv_norm.shape[1]
    rope = cache_kpe.shape[3]
    heads, nope = w_uk.shape[2], w_uk.shape[3]
    v_dim = w_uv.shape[3]
    lowrank = w_gate_up.shape[1]
    dv = gla_norm.shape[1]
    gheads = w_gla_br.shape[1] // dv
    dk = w_gate_up.shape[2] // gheads
    alpha = (2.0 * depth) ** 0.25
    assert rope + nope <= LANES and 2 * v_dim == LANES and heads % 2 == 0 and lowrank <= LANES

    splits = (q_lora, kv_lora, rope, gheads * dk, gheads * dk, gheads * dv, lowrank, gheads * dv, 2 * d)
    offs = [0]
    for s in splits:
        offs.append(offs[-1] + s)
    seg = [w_in[:, :, offs[i]:offs[i + 1]] for i in range(len(splits))]
    w_qdn, w_kvdn, w_kr, w_gq, w_gk, w_gv, w_ga, w_gr, w_br = seg
    small_cols = [w_qdn, w_kvdn, _pad_cols(w_kr, LANES), _pad_cols(_rot_cols(w_kr), LANES), _pad_cols(w_ga, LANES)]
    n_small = sum(w.shape[2] for w in small_cols)
    assert n_small % 512 == 0 and q_lora % LANES == 0 and kv_lora % LANES == 0
    w_in_r = jnp.concatenate(small_cols + [w_gq, w_gk, w_gv, w_gr, w_br], axis=2).astype(BF16)
    q0, k0 = 0, gheads * dk
    v0 = 2 * gheads * dk
    gr0 = v0 + gheads * dv
    br0 = gr0 + gheads * dv

    w_uq_h = w_uq.reshape(depth, q_lora, heads, nope + rope)
    w_q_nope, w_q_rope = w_uq_h[..., :nope], w_uq_h[..., nope:]
    zpad = jnp.zeros((depth, q_lora, heads, LANES - rope - nope), F32)
    w_q_main = jnp.concatenate([w_q_rope, w_q_nope, zpad], axis=-1)
    w_q_rot = jnp.concatenate([_rot_cols(w_q_rope), jnp.zeros_like(w_q_nope), zpad], axis=-1)
    w_uq2 = jnp.concatenate([w_q_main.reshape(depth, q_lora, heads * LANES),
                             w_q_rot.reshape(depth, q_lora, heads * LANES)], axis=-1).astype(BF16)
    w_uk_p = jnp.pad(w_uk, ((0, 0), (0, 0), (0, 0), (rope, LANES - rope - nope)))
    w_ukt_p = jnp.transpose(w_uk_p, (0, 2, 3, 1)).astype(BF16)
    w_uk_p = w_uk_p.reshape(depth, kv_lora, heads * LANES).astype(BF16)
    w_uv_f = _pad_cols(w_uv, LANES).reshape(depth, kv_lora, heads * LANES).astype(BF16)
    eye = jnp.eye(heads, dtype=F32)
    w_uv_p = (w_uv.transpose(0, 2, 1, 3)[:, :, :, None, :] * eye[None, :, None, :, None]).reshape(
        depth, heads, kv_lora, heads * v_dim).astype(BF16)
    w_gate = jnp.pad(w_gate_up, ((0, 0), (0, LANES - lowrank), (0, 0))).astype(BF16)

    def layer_weights(l):
        return dict(
            q_norm=q_norm[l][None], kv_norm=kv_norm[l][None], w_uq2=w_uq2[l], w_uk_p=w_uk_p[l], w_ukt_p=w_ukt_p[l],
            w_uv=w_uv_f[l], w_uv_p=w_uv_p[l], w_gate=w_gate[l], b_gate=b_gate[l][None],
            w_mla_br=w_mla_br[l].astype(BF16), w_gla_br=w_gla_br[l].astype(BF16), w_out=w_out[l].astype(BF16),
            ln1_g=ln1_g[l][None], ln1_b=ln1_b[l][None], w_up=w_up[l].astype(BF16), w_down=w_down[l].astype(BF16),
            ln2_g=ln2_g[l][None], ln2_b=ln2_b[l][None])

    bp, lp, _ = x_prompt.shape
    bs, ls, _ = x_sample.shape
    past_len = page_table.shape[1] * cache_ckv.shape[2]
    mod_all = _ada(jnp.concatenate([c_prompt, c_sample], axis=0), w_ada, b_ada)
    tabs_p = _rope_tables(jnp.arange(lp, dtype=jnp.int32), rope)
    tabs_s = _rope_tables(past_len + jnp.arange(ls, dtype=jnp.int32), rope)
    dims = (heads, q_lora, kv_lora, rope, nope)
    cache_kpe_t = jnp.swapaxes(cache_kpe, 2, 3)

    xp, xs = x_prompt, x_sample
    outs = {k: [] for k in ("kpe_p", "kpe_s")}
    ckv_p = jnp.zeros((depth, bp * lp, kv_lora), F32)
    ckv_s = jnp.zeros((depth, bs * ls, kv_lora), F32)
    gla_p = jnp.zeros((depth, bp, gheads, dk, dv), F32)
    gla_s = jnp.zeros((depth, bs, gheads, dk, dv), F32)
    for l in range(depth):
        wts = layer_weights(l)
        gn = gla_norm[l][None]
        mod = mod_all[l, :bp][:, None, :]
        small, big = _inproj(xp, mod, w_in_r[l], n_small)
        ckv_p, kpe, log_a, q, k, v = _prep(small, *tabs_p, bp, lp, wts, dims, False, ckv_p, l)
        o_mla = _flash(q, k, v, bp, lp, heads)
        o_gla, gla_p = _gla(big, log_a, gn, None, gla_p, l, bp, lp, gheads, dk, dv, (q0, k0, v0))
        x1 = _post(xp, mod, o_mla, o_gla, big, wts, (gr0, br0), alpha, sample=False)
        xp = _mlp(x1, mod, wts, alpha)
        outs["kpe_p"].append(kpe.reshape(bp, lp, rope))
        mod = mod_all[l, bp:][:, None, :]
        small, big = _inproj(xs, mod, w_in_r[l], n_small)
        ckv_s, kpe, log_a, qlat, qpe = _prep(small, *tabs_s, bs, ls, wts, dims, True, ckv_s, l)
        o_lat = _decode(qlat, qpe, ckv_s, kpe, cache_ckv, cache_kpe_t, page_table, l)
        o_gla, gla_s = _gla(big, log_a, gn, state_gla, gla_s, l, bs, ls, gheads, dk, dv, (q0, k0, v0))
        x1 = _post(xs, mod, o_lat, o_gla, big, wts, (gr0, br0), alpha, sample=True)
        xs = _mlp(x1, mod, wts, alpha)
        outs["kpe_s"].append(kpe.reshape(bs, ls, rope))

    kpe_p, kpe_s = jnp.stack(outs["kpe_p"]), jnp.stack(outs["kpe_s"])
    return (xp, xs, ckv_p.reshape(depth, bp, lp, kv_lora), kpe_p, gla_p,
            ckv_s.reshape(depth, bs, ls, kv_lora), kpe_s, gla_s)
```

```python
import functools

import jax
import jax.numpy as jnp
from jax import lax
from jax.experimental import pallas as pl
from jax.experimental.pallas import tpu as pltpu

F32 = jnp.float32
BF16 = jnp.bfloat16

LANES = 128
SUBLANES = 8
ROPE_THETA = 10000.0
GLA_TAU = 16.0
LOG2_E = 1.4426950408889634
GLA_CHUNK = 128
GLA_SHORT_CHUNK = 64
EPS = 1e-6
N_ADA = 6
ROW_TILE = 512
ATTN_TILE = 1024
PAGES_PER_CHUNK = 64
VMEM_LIMIT = 56 * 1024 * 1024


def _cparams(sem):
    return pltpu.CompilerParams(dimension_semantics=sem, vmem_limit_bytes=VMEM_LIMIT)


def _resident(shape):
    nd = len(shape)
    return pl.BlockSpec(shape, lambda *_: (0,) * nd, pipeline_mode=pl.Buffered(1))


def _dot(a, b):
    return jnp.dot(a, b, preferred_element_type=F32)


def _dot_nt(a, b):
    return lax.dot_general(a, b, (((1,), (1,)), ((), ())), preferred_element_type=F32)


def _ln(x):
    mu = jnp.mean(x, axis=-1, keepdims=True)
    xc = x - mu
    var = jnp.mean(xc * xc, axis=-1, keepdims=True)
    return xc * lax.rsqrt(var + EPS)


def _rms(x):
    return x * lax.rsqrt(jnp.mean(x * x, axis=-1, keepdims=True) + EPS)


def _sigmoid(x):
    return 1.0 / (1.0 + jnp.exp(-x))


def _log_sigmoid(x):
    return jnp.minimum(x, 0.0) - jnp.log(1.0 + jnp.exp(-jnp.abs(x)))


def _ada_body(c_ref, w_ref, b_ref, o_ref):
    c = c_ref[...]
    s = (c * _sigmoid(c)).astype(BF16)
    o_ref[0] = _dot(s, w_ref[0].astype(BF16)) + b_ref[0]


def _ada(c_all, w_ada, b_ada):
    depth, d, n = w_ada.shape
    rows = c_all.shape[0]
    tn = 1536
    return pl.pallas_call(
        _ada_body,
        grid=(depth, n // tn),
        in_specs=[
            pl.BlockSpec((rows, d), lambda l, j: (0, 0)),
            pl.BlockSpec((1, d, tn), lambda l, j: (l, 0, j)),
            pl.BlockSpec((1, 1, tn), lambda l, j: (l, 0, j)),
        ],
        out_specs=pl.BlockSpec((1, rows, tn), lambda l, j: (l, 0, j)),
        out_shape=jax.ShapeDtypeStruct((depth, rows, n), F32),
        compiler_params=_cparams(("arbitrary", "arbitrary")),
        name="ada_mod",
    )(c_all, w_ada, b_ada.reshape(depth, 1, n))


def _token_tiles(b, l):
    if l >= ROW_TILE:
        assert l % ROW_TILE == 0
        return 1, ROW_TILE
    bt = min(b, ROW_TILE // l)
    assert b % bt == 0
    return bt, l


def _x_spec(bt, lt, d):
    return pl.BlockSpec((bt, lt, d), lambda b, l: (b, l, 0))


def _mod_spec(bt, d, k):
    return pl.BlockSpec((bt, 1, d), lambda b, l: (b, 0, k))


def _rows_spec(bt, lt, nl, n, col=0):
    return pl.BlockSpec((bt * lt, n), lambda b, l: (b * nl + l, col))


def _inproj_body(x_ref, sc_ref, sh_ref, w_ref, small_ref, big_ref, *, n_small, chunk):
    bt, lt, d = x_ref.shape
    h = _ln(x_ref[...]) * (1.0 + sc_ref[...]) + sh_ref[...]
    hb = h.reshape(bt * lt, d).astype(BF16)
    for c0 in range(0, w_ref.shape[1], chunk):
        acc = _dot(hb, w_ref[:, c0:c0 + chunk])
        if c0 < n_small:
            small_ref[:, c0:c0 + chunk] = acc
        else:
            big_ref[:, c0 - n_small:c0 - n_small + chunk] = acc.astype(BF16)


def _inproj(x, mod, w_in_r, n_small):
    b, l, d = x.shape
    bt, lt = _token_tiles(b, l)
    nl = l // lt
    n_big = w_in_r.shape[1] - n_small
    return pl.pallas_call(
        functools.partial(_inproj_body, n_small=n_small, chunk=512),
        grid=(b // bt, nl),
        in_specs=[_x_spec(bt, lt, d), _mod_spec(bt, d, 1), _mod_spec(bt, d, 0), _resident(w_in_r.shape)],
        out_specs=[_rows_spec(bt, lt, nl, n_small), _rows_spec(bt, lt, nl, n_big)],
        out_shape=[jax.ShapeDtypeStruct((b * l, n_small), F32), jax.ShapeDtypeStruct((b * l, n_big), BF16)],
        compiler_params=_cparams(("parallel", "parallel")),
        name="in_proj",
    )(x, mod, mod, w_in_r)


def _prep_common(small_ref, cos_ref, sin_ref, qn_ref, wuq_ref, kvn_ref, wg_ref, bg_ref,
                 ckv_ref, kpe_ref, la_ref, *, q_lora, kv_lora, rope):
    rows = small_ref.shape[0]
    lt = cos_ref.shape[0]
    s = small_ref[...]
    o = q_lora + kv_lora
    q_dn, kv_dn = s[:, :q_lora], s[:, q_lora:o]
    kr, krr, ga = s[:, o:o + LANES], s[:, o + LANES:o + 2 * LANES], s[:, o + 2 * LANES:o + 3 * LANES]
    cos, sin = cos_ref[...], sin_ref[...]
    if rows != lt:
        cos = jnp.broadcast_to(cos[None], (rows // lt, lt, LANES)).reshape(rows, LANES)
        sin = jnp.broadcast_to(sin[None], (rows // lt, lt, LANES)).reshape(rows, LANES)
    qn = (_rms(q_dn) * qn_ref[...]).astype(BF16)
    q2 = _dot(qn, wuq_ref[...])
    ckv = _rms(kv_dn) * kvn_ref[...]
    ckv_ref[...] = ckv
    kpe = kr * cos + krr * sin
    kpe_ref[...] = kpe[:, :rope]
    xg = _dot(ga.astype(BF16), wg_ref[...]) + bg_ref[...]
    la_ref[...] = _log_sigmoid(xg) * (1.0 / GLA_TAU)
    return q2, ckv, kpe, cos, sin


def _prep_prompt_body(small_ref, cos_ref, sin_ref, qn_ref, wuq_ref, kvn_ref, wg_ref, bg_ref, wuk_ref, wuv_ref,
                      ckv_all_ref, ckv_ref, kpe_ref, la_ref, q_ref, k_ref, v_ref, *, heads, scale, **kw):
    del ckv_all_ref
    q2, ckv, kpe, cos, sin = _prep_common(small_ref, cos_ref, sin_ref, qn_ref, wuq_ref, kvn_ref, wg_ref, bg_ref,
                                          ckv_ref, kpe_ref, la_ref, **kw)
    cb = ckv.astype(BF16)
    kn = _dot(cb, wuk_ref[...])
    hw = heads * LANES
    v = _dot(cb, wuv_ref[...])
    upper = lax.broadcasted_iota(jnp.int32, v.shape, 1) % LANES >= LANES // 2
    v_ref[...] = jnp.where(upper, 1.0, v).astype(BF16)
    for h in range(heads):
        c0 = h * LANES
        qh = (q2[:, c0:c0 + LANES] * cos + q2[:, hw + c0:hw + c0 + LANES] * sin) * scale
        q_ref[:, c0:c0 + LANES] = qh.astype(BF16)
        k_ref[:, c0:c0 + LANES] = (kn[:, c0:c0 + LANES] + kpe).astype(BF16)


def _prep_sample_body(small_ref, cos_ref, sin_ref, qn_ref, wuq_ref, kvn_ref, wg_ref, bg_ref, wukt_ref,
                      ckv_all_ref, ckv_ref, kpe_ref, la_ref, qlat_ref, qpe_ref, *, heads, scale, rope, **kw):
    del ckv_all_ref
    q2, _, _, cos, sin = _prep_common(small_ref, cos_ref, sin_ref, qn_ref, wuq_ref, kvn_ref, wg_ref, bg_ref,
                                      ckv_ref, kpe_ref, la_ref, rope=rope, **kw)
    bt, _, lt, c_lat = qlat_ref.shape
    hw = heads * LANES
    for h in range(heads):
        c0 = h * LANES
        qh = (q2[:, c0:c0 + LANES] * cos + q2[:, hw + c0:hw + c0 + LANES] * sin) * scale
        qlat = _dot(qh.astype(BF16), wukt_ref[h])
        qlat_ref[:, h] = qlat.reshape(bt, lt, c_lat)
        qpe_ref[:, h] = qh[:, :rope].reshape(bt, lt, rope)


def _prep(small, cos_t, sin_t, b, l, wts, dims, sample, ckv_all, layer):
    heads, q_lora, kv_lora, rope, nope = dims
    bt, lt = _token_tiles(b, l)
    nl = l // lt
    rows = bt * lt
    t = b * l
    scale = float(nope + rope) ** -0.5 * (1.0 if sample else LOG2_E)
    tab = pl.BlockSpec((lt, LANES), lambda bi, li: (li, 0))
    common_w = [wts["q_norm"], wts["w_uq2"], wts["kv_norm"], wts["w_gate"], wts["b_gate"]]
    common_out_specs = [pl.BlockSpec((None, rows, kv_lora), lambda bi, li: (layer, bi * nl + li, 0)),
                        _rows_spec(bt, lt, nl, rope), _rows_spec(bt, lt, nl, wts["w_gate"].shape[1])]
    common_out_shape = [jax.ShapeDtypeStruct(ckv_all.shape, F32), jax.ShapeDtypeStruct((t, rope), F32),
                        jax.ShapeDtypeStruct((t, wts["w_gate"].shape[1]), F32)]
    kw = dict(heads=heads, scale=scale, q_lora=q_lora, kv_lora=kv_lora, rope=rope)
    if not sample:
        extra_w = [wts["w_uk_p"], wts["w_uv"]]
        hw = heads * LANES
        nv = wts["w_uv"].shape[1]
        out_specs = common_out_specs + [_rows_spec(bt, lt, nl, hw), _rows_spec(bt, lt, nl, hw),
                                        _rows_spec(bt, lt, nl, nv)]
        out_shape = common_out_shape + [jax.ShapeDtypeStruct((t, hw), BF16), jax.ShapeDtypeStruct((t, hw), BF16),
                                        jax.ShapeDtypeStruct((t, nv), BF16)]
        body = functools.partial(_prep_prompt_body, **kw)
    else:
        extra_w = [wts["w_ukt_p"]]
        out_specs = common_out_specs + [
            pl.BlockSpec((bt, heads, lt, kv_lora), lambda bi, li: (bi, 0, li, 0)),
            pl.BlockSpec((bt, heads, lt, rope), lambda bi, li: (bi, 0, li, 0))]
        out_shape = common_out_shape + [jax.ShapeDtypeStruct((b, heads, l, kv_lora), F32),
                                        jax.ShapeDtypeStruct((b, heads, l, rope), F32)]
        body = functools.partial(_prep_sample_body, **kw)
    weights = common_w + extra_w
    return pl.pallas_call(
        body,
        grid=(b // bt, nl),
        in_specs=[_rows_spec(bt, lt, nl, small.shape[1]), tab, tab] + [_resident(w.shape) for w in weights]
        + [pl.BlockSpec(memory_space=pl.ANY)],
        out_specs=out_specs,
        out_shape=out_shape,
        input_output_aliases={3 + len(weights): 0},
        compiler_params=_cparams(("parallel", "parallel")),
        name="prep_sample" if sample else "prep_prompt",
    )(small, cos_t, sin_t, *weights, ckv_all)


def _flash_body(qi_ref, kj_ref, q_ref, k_ref, v_ref, o_ref, m_ref, acc_ref, *, heads):
    i, j = qi_ref[pl.program_id(1)], kj_ref[pl.program_id(1)]
    tq, tk = q_ref.shape[1], k_ref.shape[1]
    half = LANES // 2

    @pl.when(j == 0)
    def _():
        m_ref[...] = jnp.full(m_ref.shape, -jnp.inf, F32)
        acc_ref[...] = jnp.zeros(acc_ref.shape, F32)

    def update(h, r0, nr, nc, keep):
        c0 = h * LANES
        s = _dot_nt(q_ref[0, r0:r0 + nr, c0:c0 + LANES], k_ref[0, 0:nc, c0:c0 + LANES])
        if keep is not None:
            s = jnp.where(keep, s, -jnp.inf)
        m_prev = m_ref[h, r0:r0 + nr]
        m_new = jnp.maximum(m_prev, jnp.max(s, axis=-1, keepdims=True))
        p = jnp.exp2(s - jnp.tile(m_new, (1, nc // LANES)))
        alpha = jnp.exp2(m_prev - m_new)
        acc_ref[h, r0:r0 + nr] = alpha * acc_ref[h, r0:r0 + nr] + _dot(p.astype(BF16), v_ref[0, 0:nc, c0:c0 + LANES])
        m_ref[h, r0:r0 + nr] = m_new

    def step(masked):
        keep = None
        if masked:
            keep = lax.broadcasted_iota(jnp.int32, (tq, tk), 0) >= lax.broadcasted_iota(jnp.int32, (tq, tk), 1)
        for h in range(heads):
            update(h, 0, tq, tk, keep)

    @pl.when(j < i)
    def _():
        step(False)

    @pl.when(j == i)
    def _():
        step(True)
        low = lax.broadcasted_iota(jnp.int32, (tq, LANES), 1) < half
        for hp in range(heads // 2):
            a, b = acc_ref[2 * hp], acc_ref[2 * hp + 1]
            oa = a / pltpu.roll(a, half, 1)
            ob = b / pltpu.roll(b, half, 1)
            o_ref[0, :, hp * LANES:(hp + 1) * LANES] = jnp.where(low, oa, pltpu.roll(ob, half, 1)).astype(BF16)


def _flash(q, k, v, b, l, heads):
    t = min(ATTN_TILE, l)
    n = l // t
    hw = heads * LANES
    nv = hw // 2
    q3, k3, v3 = q.reshape(b, l, hw), k.reshape(b, l, hw), v.reshape(b, l, hw)
    pairs = [(i, j) for i in range(n) for j in range(i + 1)]
    qi = jnp.asarray([p[0] for p in pairs], jnp.int32)
    kj = jnp.asarray([p[1] for p in pairs], jnp.int32)
    grid_spec = pltpu.PrefetchScalarGridSpec(
        num_scalar_prefetch=2,
        grid=(b, len(pairs)),
        in_specs=[
            pl.BlockSpec((1, t, hw), lambda bi, s, qi_, kj_: (bi, qi_[s], 0)),
            pl.BlockSpec((1, t, hw), lambda bi, s, qi_, kj_: (bi, kj_[s], 0)),
            pl.BlockSpec((1, t, hw), lambda bi, s, qi_, kj_: (bi, kj_[s], 0)),
        ],
        out_specs=pl.BlockSpec((1, t, nv), lambda bi, s, qi_, kj_: (bi, qi_[s], 0)),
        scratch_shapes=[pltpu.VMEM((heads, t, LANES), F32), pltpu.VMEM((heads, t, LANES), F32)],
    )
    out = pl.pallas_call(
        functools.partial(_flash_body, heads=heads),
        grid_spec=grid_spec,
        out_shape=jax.ShapeDtypeStruct((b, l, nv), BF16),
        compiler_params=_cparams(("parallel", "arbitrary")),
        name="flash_prompt",
    )(qi, kj, q3, k3, v3)
    return out.reshape(b * l, nv)


def _decode_body(pt_ref, qlat_ref, qpe_ref, ckvn_ref, kpen_ref, ckv_hbm, kpet_hbm, o_ref,
                 kbuf, pbuf, kb16, sbuf, kn_ref, pn_ref, sem, *, layer, n_pages, chunk, page):
    b, nb = pl.program_id(0), pl.num_programs(0)
    slot, nslot = b % 2, 1 - b % 2
    _, heads, lq, c_lat = qlat_ref.shape
    rope = qpe_ref.shape[3]
    rows = heads * lq
    n_chunks = n_pages // chunk
    width = chunk * page
    ql = qlat_ref[0].reshape(rows, c_lat).astype(BF16)
    qp = qpe_ref[0].reshape(rows, rope).astype(BF16)

    def page_copies(phys, ci, p, sl):
        r0 = pl.multiple_of((ci * chunk + p) * page, page)
        return (pltpu.make_async_copy(ckv_hbm.at[layer, phys], kbuf.at[sl, pl.ds(r0, page), :], sem.at[sl, 0, ci]),
                pltpu.make_async_copy(kpet_hbm.at[layer, phys], pbuf.at[sl, :, pl.ds(r0, page)], sem.at[sl, 1, ci]))

    @pl.when(b == 0)
    def _():
        def issue(ci, carry):
            for p in range(chunk):
                for cp in page_copies(pt_ref[ci * chunk + p], ci, p, 0):
                    cp.start()
            return carry
        lax.fori_loop(0, n_chunks, issue, 0)

    kn_ref[...] = jnp.zeros(kn_ref.shape, F32)
    pn_ref[...] = jnp.zeros(pn_ref.shape, F32)
    kn_ref[0:lq, :] = ckvn_ref[...]
    pn_ref[0:lq, :] = kpen_ref[...]
    kn = kn_ref[...].astype(BF16)
    s_new = _dot_nt(ql, kn) + _dot_nt(qp, pn_ref[...].astype(BF16))
    key = lax.broadcasted_iota(jnp.int32, (rows, LANES), 1)
    qpos = lax.broadcasted_iota(jnp.int32, (rows, LANES), 0) % lq
    s_new = jnp.where(key <= qpos, s_new, -jnp.inf)

    nxt = jnp.minimum(b + 1, nb - 1)

    def scores(i, m):
        for p in range(chunk):
            for cp in page_copies(0, i, p, slot):
                cp.wait()
        phys = [pt_ref[nxt * n_pages + i * chunk + p] for p in range(chunk)]
        for p in range(chunk):
            for cp in page_copies(phys[p], i, p, nslot):
                cp.start()
        c0 = pl.multiple_of(i * width, width)
        kb = kbuf[slot, pl.ds(c0, width), :].astype(BF16)
        kb16[pl.ds(c0, width), :] = kb
        s = _dot_nt(ql, kb) + _dot(qp, pbuf[slot, :, pl.ds(c0, width)].astype(BF16))
        sbuf[:, pl.ds(c0, width)] = s
        for t in range(width // LANES):
            m = jnp.maximum(m, s[:, t * LANES:(t + 1) * LANES])
        return m

    m = lax.fori_loop(0, n_chunks, scores, s_new)
    m = jnp.broadcast_to(jnp.max(m, axis=-1, keepdims=True), (rows, LANES))
    p_new = jnp.exp(s_new - m)
    m_wide = jnp.tile(m, (1, width // LANES))

    def values(i, carry):
        l, acc = carry
        c0 = pl.multiple_of(i * width, width)
        p = jnp.exp(sbuf[:, pl.ds(c0, width)] - m_wide)
        for t in range(width // LANES):
            l = l + p[:, t * LANES:(t + 1) * LANES]
        return l, acc + _dot(p.astype(BF16), kb16[pl.ds(c0, width), :])

    l, acc = lax.fori_loop(0, n_chunks, values, (p_new, _dot(p_new.astype(BF16), kn)))
    o = acc / jnp.sum(l, axis=-1, keepdims=True)
    o_ref[0] = o.reshape(heads, lq, c_lat)

    @pl.when(b == nb - 1)
    def _():
        def drain(ci, carry):
            for p in range(chunk):
                for cp in page_copies(0, ci, p, nslot):
                    cp.wait()
            return carry
        lax.fori_loop(0, n_chunks, drain, 0)


def _decode(qlat, qpe, ckv_new, kpe_new, cache_ckv, cache_kpe_t, page_table, layer):
    b, heads, lq, c_lat = qlat.shape
    rope = qpe.shape[3]
    n_pages = page_table.shape[1]
    page = cache_ckv.shape[2]
    chunk = min(PAGES_PER_CHUNK, n_pages)
    assert n_pages % chunk == 0 and lq % SUBLANES == 0 and lq <= LANES and page % LANES == 0
    rows = heads * lq
    t_past = n_pages * page
    any_spec = pl.BlockSpec(memory_space=pl.ANY)
    grid_spec = pltpu.PrefetchScalarGridSpec(
        num_scalar_prefetch=1,
        grid=(b,),
        in_specs=[
            pl.BlockSpec((1, heads, lq, c_lat), lambda bi, pt: (bi, 0, 0, 0)),
            pl.BlockSpec((1, heads, lq, rope), lambda bi, pt: (bi, 0, 0, 0)),
            pl.BlockSpec((None, lq, c_lat), lambda bi, pt: (layer, bi, 0)),
            pl.BlockSpec((lq, rope), lambda bi, pt: (bi, 0)),
            any_spec, any_spec,
        ],
        out_specs=pl.BlockSpec((1, heads, lq, c_lat), lambda bi, pt: (bi, 0, 0, 0)),
        scratch_shapes=[pltpu.VMEM((2, t_past, c_lat), F32), pltpu.VMEM((2, rope, t_past), F32),
                        pltpu.VMEM((t_past, c_lat), BF16), pltpu.VMEM((rows, t_past), F32),
                        pltpu.VMEM((LANES, c_lat), F32), pltpu.VMEM((LANES, rope), F32),
                        pltpu.SemaphoreType.DMA((2, 2, n_pages // chunk))],
    )
    return pl.pallas_call(
        functools.partial(_decode_body, layer=layer, n_pages=n_pages, chunk=chunk, page=page),
        grid_spec=grid_spec,
        out_shape=jax.ShapeDtypeStruct((b, heads, lq, c_lat), F32),
        compiler_params=_cparams(("arbitrary",)),
        name="decode_sample",
    )(page_table.reshape(-1), qlat, qpe, ckv_new, kpe_new, cache_ckv, cache_kpe_t)


def _gla_masks(c, seg):
    row = lax.broadcasted_iota(jnp.int32, (c, c), 0)
    col = lax.broadcasted_iota(jnp.int32, (c, c), 1)
    tri = ((row >= col) & (row // seg == col // seg)).astype(BF16)
    levels = []
    m = seg // 2
    while m >= SUBLANES:
        levels.append((m, (row // (2 * m) == col // (2 * m)) & (row % (2 * m) >= m) & (col % (2 * m) < m)))
        m //= 2
    shape3 = (c // SUBLANES, SUBLANES, c)
    sub = lax.broadcasted_iota(jnp.int32, shape3, 0)
    trow = lax.broadcasted_iota(jnp.int32, shape3, 1)
    col3 = lax.broadcasted_iota(jnp.int32, shape3, 2)
    diag = [(col3 == sub * SUBLANES + s) & (trow >= s) for s in range(SUBLANES)]
    return tri, levels, diag


def _gla_chunk_intra(q, k, v, g, masks):
    tri, levels, diag = masks
    c, dk = q.shape
    g_hi = g.astype(BF16)
    g_lo = (g - g_hi.astype(F32)).astype(BF16)
    b = _dot(tri, g_hi) + _dot(tri, g_lo)

    a = jnp.zeros((c, c), F32)
    for m, valid in levels:
        b3 = b.reshape(c // (2 * m), 2 * m, dk)
        ref = jnp.broadcast_to(b3[:, m - 1:m, :], b3.shape).reshape(c, dk)
        f = jnp.exp2(-jnp.abs(b - ref))
        pm = _dot_nt((q * f).astype(BF16), (k * f).astype(BF16))
        a = jnp.where(valid, pm, a)

    nsub = c // SUBLANES
    q3, k3, b3 = (x.reshape(nsub, SUBLANES, dk) for x in (q, k, b))
    a3 = a.reshape(nsub, SUBLANES, c)
    for s in range(SUBLANES):
        bs = jnp.broadcast_to(b3[:, s:s + 1, :], b3.shape)
        ks = jnp.broadcast_to(k3[:, s:s + 1, :], k3.shape)
        w = q3 * ks * jnp.exp2(jnp.minimum(b3 - bs, 0.0))
        a3 = jnp.where(diag[s], jnp.sum(w, axis=-1, keepdims=True), a3)
    return b, _dot(a3.reshape(c, c).astype(BF16), v.astype(BF16))


def _gla_body(*refs, c, nb, tl, seg, use_s0, scale):
    q_ref, k_ref, v_ref, g_ref, gn_ref = refs[:5]
    s0_ref = refs[5] if use_s0 else None
    o_ref, sf_ref, st_ref = refs[-3:]
    li = pl.program_id(2)
    dk, dv = q_ref.shape[1], v_ref.shape[1]

    @pl.when(li == 0)
    def _():
        for ib in range(nb):
            st_ref[ib] = s0_ref[ib, 0].T if use_s0 else jnp.zeros((dv, dk), F32)

    gn = gn_ref[...]
    n_chunks = nb * tl // c
    nseg = c // seg
    rowseg = lax.broadcasted_iota(jnp.int32, (c, 1), 0) // seg
    masks = _gla_masks(c, seg)
    for ci in range(n_chunks):
        r0 = ci * c
        q = q_ref[r0:r0 + c, :].astype(F32) * scale
        k = k_ref[r0:r0 + c, :].astype(F32)
        v = v_ref[r0:r0 + c, :].astype(F32)
        g = g_ref[r0:r0 + c, :] * LOG2_E
        b, o = _gla_chunk_intra(q, k, v, g, masks)
        b3 = b.reshape(nseg, seg, dk)
        b_last = jnp.broadcast_to(b3[:, seg - 1:seg, :], b3.shape).reshape(c, dk)
        qe = (q * jnp.exp2(b)).astype(BF16)
        ke = (k * jnp.exp2(b_last - b)).astype(BF16)
        vt = v.T.astype(BF16)
        for sg in range(nseg):
            ib = (ci * nseg + sg) if nseg > 1 else (ci * c) // tl
            st = st_ref[ib]
            if nseg > 1:
                own = rowseg == sg
                o = o + jnp.where(own, _dot_nt(qe, st.astype(BF16)), 0.0)
                ke_s = jnp.where(own, ke, jnp.zeros_like(ke))
            else:
                o = o + _dot_nt(qe, st.astype(BF16))
                ke_s = ke
            dec = jnp.exp2(b[(sg + 1) * seg - 1:(sg + 1) * seg, :])
            st_ref[ib] = st * dec + _dot(vt, ke_s)
        o_ref[r0:r0 + c, :] = (_rms(o) * gn).astype(BF16)

    @pl.when(li == pl.num_programs(2) - 1)
    def _():
        for ib in range(nb):
            sf_ref[ib, 0] = st_ref[ib].T


def _gla(big, log_a, gla_norm, s0_all, s_out_all, layer, b, l, heads, dk, dv, cols):
    c = GLA_CHUNK if l >= GLA_CHUNK else GLA_SHORT_CHUNK
    if l >= c:
        assert l % c == 0
        nb, seg = 1, c
        tl = min(ROW_TILE, l)
    else:
        assert c % l == 0 and l % SUBLANES == 0
        nb, seg, tl = c // l, l, l
        assert b % nb == 0
    nl = l // tl
    rows = nb * tl
    q0, k0, v0 = cols

    def tok(width, base):
        return pl.BlockSpec((rows, width), lambda bi, h, li: (bi * nl + li, base // width + h))

    in_specs = [tok(dk, q0), tok(dk, k0), tok(dv, v0), tok(dk, 0), pl.BlockSpec((1, dv), lambda bi, h, li: (0, 0))]
    args = [big, big, big, log_a, gla_norm]
    state_spec = pl.BlockSpec((None, nb, 1, dk, dv), lambda bi, h, li: (layer, bi, h, 0, 0))
    if s0_all is not None:
        in_specs.append(state_spec)
        args.append(s0_all)
    in_specs.append(pl.BlockSpec(memory_space=pl.ANY))
    args.append(s_out_all)
    return pl.pallas_call(
        functools.partial(_gla_body, c=c, nb=nb, tl=tl, seg=seg, use_s0=s0_all is not None, scale=float(dk) ** -0.5),
        grid=(b // nb, heads, nl),
        in_specs=in_specs,
        out_specs=[pl.BlockSpec((rows, dv), lambda bi, h, li: (bi * nl + li, h)), state_spec],
        out_shape=[jax.ShapeDtypeStruct((b * l, heads * dv), BF16), jax.ShapeDtypeStruct(s_out_all.shape, F32)],
        input_output_aliases={len(args) - 1: 1},
        scratch_shapes=[pltpu.VMEM((nb, dv, dk), F32)],
        compiler_params=_cparams(("parallel", "parallel", "arbitrary")),
        name="gla",
    )(*args)


def _post_body(*refs, sample, alpha):
    if sample:
        (x_ref, gt_ref, olat_ref, ogla_ref, gr_ref, bra_ref, brb_ref, wuv_ref,
         wmla_ref, wgla_ref, wout_ref, g_ref, b_ref, o_ref) = refs
        bt, heads, lt, c_lat = olat_ref.shape
        o_mla = _dot(olat_ref[:, 0].reshape(bt * lt, c_lat).astype(BF16), wuv_ref[0])
        for h in range(1, heads):
            o_mla = o_mla + _dot(olat_ref[:, h].reshape(bt * lt, c_lat).astype(BF16), wuv_ref[h])
        o_mla = o_mla.astype(BF16)
    else:
        (x_ref, gt_ref, omla_ref, ogla_ref, gr_ref, bra_ref, brb_ref,
         wmla_ref, wgla_ref, wout_ref, g_ref, b_ref, o_ref) = refs
        o_mla = omla_ref[...]
    bt, lt, d = x_ref.shape
    u_mla = _dot(o_mla, wmla_ref[...])
    gr = gr_ref[...].astype(F32)
    og = (ogla_ref[...].astype(F32) * (gr * _sigmoid(gr))).astype(BF16)
    u_gla = _dot(og, wgla_ref[...])
    merged = _sigmoid(bra_ref[...].astype(F32)) * u_mla + _sigmoid(brb_ref[...].astype(F32)) * u_gla
    mix = _dot(merged.astype(BF16), wout_ref[...]).reshape(bt, lt, d)
    y = alpha * x_ref[...] + (1.0 + gt_ref[...]) * mix
    o_ref[...] = _ln(y) * g_ref[...] + b_ref[...]


def _post(x, mod, o_mla, o_gla, big, wts, cols, alpha, sample):
    b, l, d = x.shape
    bt, lt = _token_tiles(b, l)
    nl = l // lt
    gr0, br0 = cols
    if sample:
        heads, c_lat = o_mla.shape[1], o_mla.shape[3]
        mla_spec = pl.BlockSpec((bt, heads, lt, c_lat), lambda bi, li: (bi, 0, li, 0))
        weights = [wts["w_uv_p"], wts["w_mla_br"], wts["w_gla_br"], wts["w_out"], wts["ln1_g"], wts["ln1_b"]]
    else:
        mla_spec = _rows_spec(bt, lt, nl, o_mla.shape[1])
        weights = [wts["w_mla_br"], wts["w_gla_br"], wts["w_out"], wts["ln1_g"], wts["ln1_b"]]
    in_specs = [_x_spec(bt, lt, d), _mod_spec(bt, d, 2), mla_spec, _rows_spec(bt, lt, nl, o_gla.shape[1]),
                _rows_spec(bt, lt, nl, d, gr0 // d), _rows_spec(bt, lt, nl, d, br0 // d),
                _rows_spec(bt, lt, nl, d, br0 // d + 1)]
    return pl.pallas_call(
        functools.partial(_post_body, sample=sample, alpha=alpha),
        grid=(b // bt, nl),
        in_specs=in_specs + [_resident(w.shape) for w in weights],
        out_specs=_x_spec(bt, lt, d),
        out_shape=jax.ShapeDtypeStruct((b, l, d), F32),
        compiler_params=_cparams(("parallel", "parallel")),
        name="merge_sample" if sample else "merge_prompt",
    )(x, mod, o_mla, o_gla, big, big, big, *weights)


def _mlp_body(x_ref, sc_ref, sh_ref, gt_ref, wup_ref, wdn_ref, g_ref, b_ref, o_ref, *, alpha, chunk):
    bt, lt, d = x_ref.shape
    x = x_ref[...]
    hb = (_ln(x) * (1.0 + sc_ref[...]) + sh_ref[...]).reshape(bt * lt, d).astype(BF16)
    f = jnp.zeros((bt * lt, d), F32)
    for c0 in range(0, wup_ref.shape[1], chunk):
        u = jnp.maximum(_dot(hb, wup_ref[:, c0:c0 + chunk]), 0.0)
        f = f + _dot((u * u).astype(BF16), wdn_ref[c0:c0 + chunk, :])
    y = alpha * x + (1.0 + gt_ref[...]) * f.reshape(bt, lt, d)
    o_ref[...] = _ln(y) * g_ref[...] + b_ref[...]


def _mlp(x, mod, wts, alpha):
    b, l, d = x.shape
    bt, lt = _token_tiles(b, l)
    weights = [wts["w_up"], wts["w_down"], wts["ln2_g"], wts["ln2_b"]]
    return pl.pallas_call(
        functools.partial(_mlp_body, alpha=alpha, chunk=1024),
        grid=(b // bt, l // lt),
        in_specs=[_x_spec(bt, lt, d), _mod_spec(bt, d, 4), _mod_spec(bt, d, 3), _mod_spec(bt, d, 5)]
        + [_resident(w.shape) for w in weights],
        out_specs=_x_spec(bt, lt, d),
        out_shape=jax.ShapeDtypeStruct((b, l, d), F32),
        compiler_params=_cparams(("parallel", "parallel")),
        name="mlp",
    )(x, mod, mod, mod, *weights)


def _rot_cols(w):
    half = w.shape[-1] // 2
    return jnp.concatenate([-w[..., half:], w[..., :half]], axis=-1)


def _pad_cols(w, width):
    return jnp.pad(w, [(0, 0)] * (w.ndim - 1) + [(0, width - w.shape[-1])])


def _rope_tables(pos, rope):
    half = rope // 2
    inv = ROPE_THETA ** (-jnp.arange(half, dtype=F32) / half)
    ang = pos.astype(F32)[:, None] * inv[None, :]
    cos, sin = jnp.cos(ang), jnp.sin(ang)
    n = pos.shape[0]
    cos_t = jnp.concatenate([cos, cos, jnp.ones((n, LANES - 2 * rope), F32), jnp.zeros((n, rope), F32)], axis=1)
    sin_t = jnp.concatenate([sin, sin, jnp.zeros((n, LANES - rope), F32)], axis=1)
    return cos_t, sin_t


def kernel(x_prompt, x_sample, cache_ckv, cache_kpe, state_gla, page_table, c_prompt, c_sample, w_ada, b_ada, w_in, q_norm, kv_norm, w_uq, w_uk, w_uv, w_mla_br, w_gate_up, b_gate, gla_norm, w_gla_br, w_out, ln1_g, ln1_b, w_up, w_down, ln2_g, ln2_b):
    depth, d, _ = w_in.shape
    q_lora, kv_lora = q_norm.shape[1], kv_norm.shape[1]
    rope = cache_kpe.shape[3]
    heads, nope = w_uk.shape[2], w_uk.shape[3]
    v_dim = w_uv.shape[3]
    lowrank = w_gate_up.shape[1]
    dv = gla_norm.shape[1]
    gheads = w_gla_br.shape[1] // dv
    dk = w_gate_up.shape[2] // gheads
    alpha = (2.0 * depth) ** 0.25
    assert rope + nope <= LANES and 2 * v_dim == LANES and heads % 2 == 0 and lowrank <= LANES

    splits = (q_lora, kv_lora, rope, gheads * dk, gheads * dk, gheads * dv, lowrank, gheads * dv, 2 * d)
    offs = [0]
    for s in splits:
        offs.append(offs[-1] + s)
    seg = [w_in[:, :, offs[i]:offs[i + 1]] for i in range(len(splits))]
    w_qdn, w_kvdn, w_kr, w_gq, w_gk, w_gv, w_ga, w_gr, w_br = seg
    small_cols = [w_qdn, w_kvdn, _pad_cols(w_kr, LANES), _pad_cols(_rot_cols(w_kr), LANES), _pad_cols(w_ga, LANES)]
    n_small = sum(w.shape[2] for w in small_cols)
    assert n_small % 512 == 0 and q_lora % LANES == 0 and kv_lora % LANES == 0
    w_in_r = jnp.concatenate(small_cols + [w_gq, w_gk, w_gv, w_gr, w_br], axis=2).astype(BF16)
    q0, k0 = 0, gheads * dk
    v0 = 2 * gheads * dk
    gr0 = v0 + gheads * dv
    br0 = gr0 + gheads * dv

    w_uq_h = w_uq.reshape(depth, q_lora, heads, nope + rope)
    w_q_nope, w_q_rope = w_uq_h[..., :nope], w_uq_h[..., nope:]
    zpad = jnp.zeros((depth, q_lora, heads, LANES - rope - nope), F32)
    w_q_main = jnp.concatenate([w_q_rope, w_q_nope, zpad], axis=-1)
    w_q_rot = jnp.concatenate([_rot_cols(w_q_rope), jnp.zeros_like(w_q_nope), zpad], axis=-1)
    w_uq2 = jnp.concatenate([w_q_main.reshape(depth, q_lora, heads * LANES),
                             w_q_rot.reshape(depth, q_lora, heads * LANES)], axis=-1).astype(BF16)
    w_uk_p = jnp.pad(w_uk, ((0, 0), (0, 0), (0, 0), (rope, LANES - rope - nope)))
    w_ukt_p = jnp.transpose(w_uk_p, (0, 2, 3, 1)).astype(BF16)
    w_uk_p = w_uk_p.reshape(depth, kv_lora, heads * LANES).astype(BF16)
    w_uv_f = _pad_cols(w_uv, LANES).reshape(depth, kv_lora, heads * LANES).astype(BF16)
    eye = jnp.eye(heads, dtype=F32)
    w_uv_p = (w_uv.transpose(0, 2, 1, 3)[:, :, :, None, :] * eye[None, :, None, :, None]).reshape(
        depth, heads, kv_lora, heads * v_dim).astype(BF16)
    w_gate = jnp.pad(w_gate_up, ((0, 0), (0, LANES - lowrank), (0, 0))).astype(BF16)

    def layer_weights(l):
        return dict(
            q_norm=q_norm[l][None], kv_norm=kv_norm[l][None], w_uq2=w_uq2[l], w_uk_p=w_uk_p[l], w_ukt_p=w_ukt_p[l],
            w_uv=w_uv_f[l], w_uv_p=w_uv_p[l], w_gate=w_gate[l], b_gate=b_gate[l][None],
            w_mla_br=w_mla_br[l].astype(BF16), w_gla_br=w_gla_br[l].astype(BF16), w_out=w_out[l].astype(BF16),
            ln1_g=ln1_g[l][None], ln1_b=ln1_b[l][None], w_up=w_up[l].astype(BF16), w_down=w_down[l].astype(BF16),
            ln2_g=ln2_g[l][None], ln2_b=ln2_b[l][None])

    bp, lp, _ = x_prompt.shape
    bs, ls, _ = x_sample.shape
    past_len = page_table.shape[1] * cache_ckv.shape[2]
    mod_all = _ada(jnp.concatenate([c_prompt, c_sample], axis=0), w_ada, b_ada)
    tabs_p = _rope_tables(jnp.arange(lp, dtype=jnp.int32), rope)
    tabs_s = _rope_tables(past_len + jnp.arange(ls, dtype=jnp.int32), rope)
    dims = (heads, q_lora, kv_lora, rope, nope)
    cache_kpe_t = jnp.swapaxes(cache_kpe, 2, 3)

    xp, xs = x_prompt, x_sample
    outs = {k: [] for k in ("kpe_p", "kpe_s")}
    ckv_p = jnp.zeros((depth, bp * lp, kv_lora), F32)
    ckv_s = jnp.zeros((depth, bs * ls, kv_lora), F32)
    gla_p = jnp.zeros((depth, bp, gheads, dk, dv), F32)
    gla_s = jnp.zeros((depth, bs, gheads, dk, dv), F32)
    for l in range(depth):
        wts = layer_weights(l)
        gn = gla_norm[l][None]
        mod = mod_all[l, :bp][:, None, :]
        small, big = _inproj(xp, mod, w_in_r[l], n_small)
        ckv_p, kpe, log_a, q, k, v = _prep(small, *tabs_p, bp, lp, wts, dims, False, ckv_p, l)
        o_mla = _flash(q, k, v, bp, lp, heads)
        o_gla, gla_p = _gla(big, log_a, gn, None, gla_p, l, bp, lp, gheads, dk, dv, (q0, k0, v0))
        x1 = _post(xp, mod, o_mla, o_gla, big, wts, (gr0, br0), alpha, sample=False)
        xp = _mlp(x1, mod, wts, alpha)
        outs["kpe_p"].append(kpe.reshape(bp, lp, rope))
        mod = mod_all[l, bp:][:, None, :]
        small, big = _inproj(xs, mod, w_in_r[l], n_small)
        ckv_s, kpe, log_a, qlat, qpe = _prep(small, *tabs_s, bs, ls, wts, dims, True, ckv_s, l)
        o_lat = _decode(qlat, qpe, ckv_s, kpe, cache_ckv, cache_kpe_t, page_table, l)
        o_gla, gla_s = _gla(big, log_a, gn, state_gla, gla_s, l, bs, ls, gheads, dk, dv, (q0, k0, v0))
        x1 = _post(xs, mod, o_lat, o_gla, big, wts, (gr0, br0), alpha, sample=True)
        xs = _mlp(x1, mod, wts, alpha)
        outs["kpe_s"].append(kpe.reshape(bs, ls, rope))

    kpe_p, kpe_s = jnp.stack(outs["kpe_p"]), jnp.stack(outs["kpe_s"])
    return (xp, xs, ckv_p.reshape(depth, bp, lp, kv_lora), kpe_p, gla_p,
            ckv_s.reshape(depth, bs, ls, kv_lora), kpe_s, gla_s)
```
